```python
import math
import jax
import jax.numpy as jnp
from jax import lax
import numpy as np

D_MODEL = 1024
BATCH = 1
SEQ = 16384
DEPTH = 2
DEC_BATCH = 32
DEC_SEQ = 4
PAST_LEN = 16384
PAGE_SIZE = 128

D_MIX = D_MODEL
N_GROUPS = 4
GROUP_W = D_MIX // N_GROUPS
HEAD_DIM = 64
FOX_HEADS = GROUP_W // HEAD_DIM
HGRN_HEADS = GROUP_W // HEAD_DIM
HGRN_CHUNK = 64
NSA_HEADS = GROUP_W // HEAD_DIM
NSA_KV_ROWS = 6
NSA_CMP_BLOCK = 32
NSA_SEL_BLOCK = 64
NSA_TOPK = 16
NSA_WINDOW = 512
LRU_WIDTH = GROUP_W
LRU_BLOCKS = 4
LRU_CONV = 4
LRU_C = 8.0
MEM_LEN = 256
XA_HEADS = 4
XA_HEAD_DIM = 64
D_FF = 4 * D_MODEL
Q_BLOCK = 128
N_NORMS = 6
EPS = 1e-6
NEG_BIG = -1e30
SEL_FORCED = 1e9
SEL_INVALID = -1e9
LB_FLOOR = 1e-30
SPLIT_SIZES = (GROUP_W, GROUP_W, GROUP_W, FOX_HEADS,
               GROUP_W, GROUP_W, GROUP_W, GROUP_W,
               GROUP_W, NSA_KV_ROWS * HEAD_DIM, 3 * NSA_HEADS,
               LRU_WIDTH, LRU_WIDTH)
D_PROJ = sum(SPLIT_SIZES)

kernel_name = 'hymba_fox_hgrn2_nsa_rglru_step'

F32 = jnp.float32


def _rms(x, g):
    xf = x.astype(F32)
    y = xf * lax.rsqrt(jnp.mean(xf * xf, axis=-1, keepdims=True) + EPS)
    return (y * g.astype(F32)).astype(x.dtype)


def _split_proj(z):
    idx = np.cumsum(SPLIT_SIZES)[:-1].tolist()
    return jnp.split(z, idx, axis=-1)


def _masked_softmax(s, mask):
    s = jnp.where(mask, s.astype(F32), NEG_BIG)
    m = jnp.max(s, axis=-1, keepdims=True)
    e = jnp.where(mask, jnp.exp(s - m), 0.0)
    d = jnp.sum(e, axis=-1, keepdims=True)
    return e / jnp.where(d > 0, d, 1.0)


def _to_blocks(a, nb):
    return jnp.moveaxis(a.reshape((a.shape[0], nb, Q_BLOCK) + a.shape[2:]), 1, 0)


def _from_blocks(a):
    a = jnp.moveaxis(a, 0, 1)
    return a.reshape((a.shape[0], a.shape[1] * a.shape[2]) + a.shape[3:])


def _fox_attend(q, q_pos, cq, k, v, ck):
    s = jnp.einsum('bqhd,bshd->bhqs', q, k).astype(F32) * (HEAD_DIM ** -0.5)
    s = s + jnp.transpose(cq, (0, 2, 1))[..., None] - jnp.transpose(ck, (0, 2, 1))[:, :, None, :]
    mask = jnp.arange(k.shape[1])[None, :] <= q_pos[:, None]
    p = _masked_softmax(s, mask)
    return jnp.einsum('bhqs,bshd->bqhd', p.astype(v.dtype), v)


def _fox_prompt(q, k, v, logf):
    B, L = q.shape[:2]
    nb = L // Q_BLOCK
    c = jnp.cumsum(logf.astype(F32), axis=1)
    pos = jnp.arange(L).reshape(nb, Q_BLOCK)

    def blk(args):
        qb, pb, cb = args
        return _fox_attend(qb, pb, cb, k, v, c)

    return _from_blocks(lax.map(blk, (_to_blocks(q, nb), pos, _to_blocks(c, nb))))


def _fox_sample(q, kv_new, logf_new, past_kv, past_logf):
    P, T = past_kv.shape[1], q.shape[1]
    kv = jnp.concatenate([past_kv.astype(kv_new.dtype), kv_new], axis=1)
    c = jnp.cumsum(jnp.concatenate([past_logf.astype(F32), logf_new], axis=1), axis=1)
    pos = P + jnp.arange(T)
    return _fox_attend(q, pos, c[:, P:], kv[:, :, 0], kv[:, :, 1], c)


def _gla_chunked(q, k, v, log_f, S0):
    B, L, H, DK = q.shape
    DV = v.shape[-1]
    C = math.gcd(L, HGRN_CHUNK)
    n = L // C

    def chunks(a):
        return jnp.transpose(a.reshape(B, n, C, H, a.shape[-1]), (1, 0, 3, 2, 4))

    causal = jnp.tril(jnp.ones((C, C), dtype=bool))[:, :, None]

    def step(S, inp):
        qc, kc, vc, gc = inp
        b = jnp.cumsum(gc, axis=2)
        rel = b[:, :, :, None, :] - b[:, :, None, :, :]
        decay = jnp.where(causal, jnp.exp(jnp.where(causal, rel, 0.0)), 0.0)
        attn = jnp.einsum('bhtk,bhsk,bhtsk->bhts', qc, kc, decay)
        o = (jnp.einsum('bhts,bhsv->bhtv', attn, vc)
             + jnp.einsum('bhtk,bhkv->bhtv', qc * jnp.exp(b), S))
        b_last = b[:, :, -1, :]
        S = (jnp.exp(b_last)[..., None] * S
             + jnp.einsum('bhsk,bhsv->bhkv', kc * jnp.exp(b_last[:, :, None, :] - b), vc))
        return S, o

    S, o = lax.scan(step, S0, (chunks(q), chunks(k), chunks(v), chunks(log_f)))
    return jnp.transpose(o, (1, 0, 3, 2, 4)).reshape(B, L, H, DV), S


def _hgrn2(q_raw, f_raw, i_raw, g_raw, lb, gain, S0):
    B, L, _ = q_raw.shape
    heads = lambda a: a.reshape(B, L, HGRN_HEADS, HEAD_DIM)
    q = jax.nn.silu(heads(q_raw)).astype(F32)
    z = heads(f_raw).astype(F32)
    lbh = lb.reshape(HGRN_HEADS, HEAD_DIM).astype(F32)
    log_f = jnp.logaddexp(jnp.log1p(-lbh) + jax.nn.log_sigmoid(z), jnp.log(jnp.maximum(lbh, LB_FLOOR)))
    k = (1.0 - lbh) * jax.nn.sigmoid(-z)
    o, S = _gla_chunked(q, k, heads(i_raw).astype(F32), log_f, S0.astype(F32))
    o = _rms(o, gain.reshape(HGRN_HEADS, HEAD_DIM)).reshape(B, L, GROUP_W)
    return o.astype(q_raw.dtype) * jax.nn.silu(g_raw), S


def _nsa_compress(kc, vc, w_cmp):
    B, L, D = kc.shape
    N = L // NSA_CMP_BLOCK
    blocks = lambda a: a[:, :N * NSA_CMP_BLOCK].reshape(B, N, NSA_CMP_BLOCK, D)
    kcmp = jnp.einsum('bnjd,jd->bnd', blocks(kc), w_cmp[0])
    vcmp = jnp.einsum('bnjd,jd->bnd', blocks(vc), w_cmp[1])
    cmp_end = (jnp.arange(N) + 1) * NSA_CMP_BLOCK - 1
    return kcmp, vcmp, cmp_end


def _sel_blocks(a):
    B, L, D = a.shape
    NS = -(-L // NSA_SEL_BLOCK)
    a = jnp.pad(a, ((0, 0), (0, NS * NSA_SEL_BLOCK - L), (0, 0)))
    return a.reshape(B, NS, NSA_SEL_BLOCK, D)


def _nsa_core(q, pos, gates, kcmp, vcmp, cmp_end, ks_blk, vs_blk, kw, vw, kw_pos):
    B, Q, H, D = q.shape
    N, NS = kcmp.shape[1], ks_blk.shape[1]
    scale = D ** -0.5
    s_c = jnp.einsum('bqhd,bnd->bhqn', q, kcmp) * scale
    p_c = _masked_softmax(s_c, cmp_end[None, :] <= pos[:, None])
    o_c = jnp.einsum('bhqn,bnd->bqhd', p_c.astype(q.dtype), vcmp)
    ratio = NSA_SEL_BLOCK // NSA_CMP_BLOCK
    imp = jnp.sum(p_c, axis=1)
    imp = jnp.pad(imp, ((0, 0), (0, 0), (0, NS * ratio - N))).reshape(B, Q, NS, ratio).sum(-1)
    blk = jnp.arange(NS)
    forced = (blk[None, :] == pos[:, None] // NSA_SEL_BLOCK) | (blk[None, :] == 0)
    causal = blk[None, :] * NSA_SEL_BLOCK <= pos[:, None]
    imp = jnp.where(forced, SEL_FORCED, jnp.where(causal, imp, SEL_INVALID))
    n_top = min(NSA_TOPK, NS)
    top_val, top_idx = lax.top_k(imp, n_top)
    ks_g = jax.vmap(lambda kb, ix: kb[ix])(ks_blk, top_idx)
    vs_g = jax.vmap(lambda vb, ix: vb[ix])(vs_blk, top_idx)
    tok = top_idx[..., None] * NSA_SEL_BLOCK + jnp.arange(NSA_SEL_BLOCK)
    m_s = (top_val[..., None] >= 0.0) & (tok <= pos[None, :, None, None])
    s_s = jnp.einsum('bqhd,bqkjd->bhqkj', q, ks_g) * scale
    p_s = _masked_softmax(s_s.reshape(B, H, Q, -1), m_s.reshape(B, 1, Q, -1))
    o_s = jnp.einsum('bhqm,bqmd->bqhd', p_s.astype(q.dtype), vs_g.reshape(B, Q, -1, D))
    s_w = jnp.einsum('bqhd,bkd->bhqk', q, kw) * scale
    dist = pos[:, None] - kw_pos[None, :]
    p_w = _masked_softmax(s_w, (dist >= 0) & (dist < NSA_WINDOW) & (kw_pos[None, :] >= 0))
    o_w = jnp.einsum('bhqk,bkd->bqhd', p_w.astype(q.dtype), vw)
    return gates[..., 0:1] * o_c + gates[..., 1:2] * o_s + gates[..., 2:3] * o_w


def _nsa_prompt(q, kv6, gates, w_cmp):
    B, L = q.shape[:2]
    nb = L // Q_BLOCK
    kcmp, vcmp, cmp_end = _nsa_compress(kv6[:, :, 0], kv6[:, :, 1], w_cmp)
    ks_blk, vs_blk = _sel_blocks(kv6[:, :, 2]), _sel_blocks(kv6[:, :, 3])
    pad = ((0, 0), (NSA_WINDOW, 0), (0, 0))
    kw_pad, vw_pad = jnp.pad(kv6[:, :, 4], pad), jnp.pad(kv6[:, :, 5], pad)
    pos = jnp.arange(L).reshape(nb, Q_BLOCK)

    def blk(args):
        qb, pb, gb = args
        start = pb[0]
        kw = lax.dynamic_slice_in_dim(kw_pad, start, NSA_WINDOW + Q_BLOCK, axis=1)
        vw = lax.dynamic_slice_in_dim(vw_pad, start, NSA_WINDOW + Q_BLOCK, axis=1)
        kw_pos = start - NSA_WINDOW + jnp.arange(NSA_WINDOW + Q_BLOCK)
        return _nsa_core(qb, pb, gb, kcmp, vcmp, cmp_end, ks_blk, vs_blk, kw, vw, kw_pos)

    return _from_blocks(lax.map(blk, (_to_blocks(q, nb), pos, _to_blocks(gates, nb))))


def _nsa_sample(q, kv6, gates, w_cmp, past_kv4, win_buf):
    P, T, WB = past_kv4.shape[1], q.shape[1], win_buf.shape[1]
    full = jnp.concatenate([past_kv4.astype(kv6.dtype), kv6[:, :, :4]], axis=1)
    kcmp, vcmp, cmp_end = _nsa_compress(full[:, :, 0], full[:, :, 1], w_cmp)
    ks_blk, vs_blk = _sel_blocks(full[:, :, 2]), _sel_blocks(full[:, :, 3])
    win = jnp.concatenate([win_buf.astype(kv6.dtype), kv6[:, :, 4:]], axis=1)
    kw_pos = P - WB + jnp.arange(WB + T)
    pos = P + jnp.arange(T)
    o = _nsa_core(q, pos, gates, kcmp, vcmp, cmp_end, ks_blk, vs_blk, win[:, :, 0], win[:, :, 1], kw_pos)
    return o, win[:, -WB:]


def _lin_comb(e1, e2):
    return e1[0] * e2[0], e2[0] * e1[1] + e2[1]


def _rglru(u_raw, gate, conv_buf, h0, W, l):
    B, L, _ = u_raw.shape
    cw = W['lru_conv_w'][l]
    xp = jnp.concatenate([conv_buf.astype(u_raw.dtype), u_raw], axis=1)
    u = W['lru_conv_b'][l] + xp[:, 0:L] * cw[0]
    for j in range(1, LRU_CONV):
        u = u + xp[:, j:j + L] * cw[j]
    ub = u.reshape(B, L, LRU_BLOCKS, LRU_WIDTH // LRU_BLOCKS)
    r = jax.nn.sigmoid(jnp.einsum('blnc,ncd->blnd', ub, W['lru_wa'][l]).reshape(B, L, LRU_WIDTH) + W['lru_ba'][l])
    i = jax.nn.sigmoid(jnp.einsum('blnc,ncd->blnd', ub, W['lru_wx'][l]).reshape(B, L, LRU_WIDTH) + W['lru_bx'][l])
    log_a = -LRU_C * r.astype(F32) * jax.nn.softplus(-W['lru_lambda'][l].astype(F32))
    a = jnp.exp(log_a)
    b = jnp.sqrt(-jnp.expm1(2.0 * log_a)) * (i * u).astype(F32)
    b = b.at[:, 0].add(a[:, 0] * h0.astype(F32))
    _, h = lax.associative_scan(_lin_comb, (a, b), axis=1)
    y = h.astype(u_raw.dtype) * jax.nn.gelu(gate)
    return y, xp[:, L:], h[:, -1]


def _project(x, W, l):
    h = _rms(x, W['norms'][l, 0])
    return _split_proj(h @ W['w_in'][l])


def _mix_residual(x, o_a, o_b, o_c, o_d, W, l):
    B, L, _ = x.shape
    gm = W['g_mix'][l]
    groups = jnp.concatenate([
        _rms(o_a.reshape(B, L, GROUP_W), gm[:GROUP_W]),
        o_b.astype(x.dtype),
        _rms(o_c.reshape(B, L, GROUP_W), gm[2 * GROUP_W:3 * GROUP_W]),
        _rms(o_d, gm[3 * GROUP_W:])], axis=-1)
    y = groups @ W['w_out'][l]
    return x + _rms(y, W['norms'][l, 1])


def _mem_kv(mem, W, l):
    m = _rms(mem, W['g_mem'][l])
    return (m @ W['xa_wkv'][l]).reshape(mem.shape[0], mem.shape[1], 2, XA_HEADS, XA_HEAD_DIM)


def _cross_and_ffn(x, xa_kv, W, l):
    B, L, _ = x.shape
    h = _rms(x, W['norms'][l, 2])
    q = (h @ W['xa_wq'][l]).reshape(B, L, XA_HEADS, XA_HEAD_DIM)
    kv = xa_kv.astype(x.dtype)
    s = jnp.einsum('bqhd,bmhd->bhqm', q, kv[:, :, 0]).astype(F32) * (XA_HEAD_DIM ** -0.5)
    p = jax.nn.softmax(s, axis=-1)
    o = jnp.einsum('bhqm,bmhd->bqhd', p.astype(x.dtype), kv[:, :, 1]).reshape(B, L, -1) @ W['xa_wo'][l]
    x = x + _rms(o, W['norms'][l, 3])
    h = _rms(x, W['norms'][l, 4])
    f = jnp.square(jax.nn.relu(h @ W['w_ff1'][l])) @ W['w_ff2'][l]
    return x + _rms(f, W['norms'][l, 5])


def _prompt_layer(x, mem, W, l, lb):
    B, L, _ = x.shape
    fq, fk, fv, ff, hq, hf, hi, hg, nq, nkv, ng, ux, ug = _project(x, W, l)
    heads = lambda a: a.reshape(B, L, -1, HEAD_DIM)
    fox_logf = jax.nn.log_sigmoid((ff + W['b_fox_f'][l]).astype(F32))
    fox_kv = jnp.stack([heads(fk), heads(fv)], axis=2)
    o_a = _fox_prompt(heads(fq), fox_kv[:, :, 0], fox_kv[:, :, 1], fox_logf)
    S0 = jnp.zeros((B, HGRN_HEADS, HEAD_DIM, HEAD_DIM), F32)
    o_b, S = _hgrn2(hq, hf, hi, hg, lb, W['g_mix'][l, GROUP_W:2 * GROUP_W], S0)
    nkv = nkv.reshape(B, L, NSA_KV_ROWS, HEAD_DIM)
    gates = jax.nn.sigmoid(ng.reshape(B, L, NSA_HEADS, 3))
    o_c = _nsa_prompt(heads(nq), nkv, gates, W['nsa_cmp_w'][l])
    wb = min(NSA_WINDOW, L)
    nsa_win = nkv[:, L - wb:, 4:]
    buf0 = jnp.zeros((B, LRU_CONV - 1, LRU_WIDTH), x.dtype)
    h0 = jnp.zeros((B, LRU_WIDTH), F32)
    o_d, conv_buf, h_last = _rglru(ux, ug, buf0, h0, W, l)
    x = _mix_residual(x, o_a, o_b, o_c, o_d, W, l)
    xa_kv = _mem_kv(mem, W, l)
    x = _cross_and_ffn(x, xa_kv, W, l)
    return x, (fox_kv, fox_logf, nkv[:, :, :4], nsa_win, S, h_last, conv_buf, xa_kv)


def _sample_layer(x, W, l, lb, page_table, fox_kv_pages, fox_logf_pages, nsa_pages,
                  win_buf, S0, h0, conv_buf, xa_kv):
    B, T, _ = x.shape
    fq, fk, fv, ff, hq, hf, hi, hg, nq, nkv, ng, ux, ug = _project(x, W, l)
    heads = lambda a: a.reshape(B, T, -1, HEAD_DIM)
    fox_logf = jax.nn.log_sigmoid((ff + W['b_fox_f'][l]).astype(F32))
    fox_kv = jnp.stack([heads(fk), heads(fv)], axis=2)
    past_kv = fox_kv_pages[page_table].reshape((B, -1) + fox_kv_pages.shape[2:])
    past_logf = fox_logf_pages[page_table].reshape(B, -1, FOX_HEADS)
    o_a = _fox_sample(heads(fq), fox_kv, fox_logf, past_kv, past_logf)
    o_b, S = _hgrn2(hq, hf, hi, hg, lb, W['g_mix'][l, GROUP_W:2 * GROUP_W], S0)
    nkv = nkv.reshape(B, T, NSA_KV_ROWS, HEAD_DIM)
    gates = jax.nn.sigmoid(ng.reshape(B, T, NSA_HEADS, 3))
    past_nsa = nsa_pages[page_table].reshape((B, -1) + nsa_pages.shape[2:])
    o_c, new_win = _nsa_sample(heads(nq), nkv, gates, W['nsa_cmp_w'][l], past_nsa, win_buf)
    o_d, new_conv, h_last = _rglru(ux, ug, conv_buf, h0, W, l)
    x = _mix_residual(x, o_a, o_b, o_c, o_d, W, l)
    x = _cross_and_ffn(x, xa_kv, W, l)
    return x, (fox_kv, fox_logf, nkv[:, :, :4], new_win, S, h_last, new_conv)


def _stack(outs, i):
    return jnp.stack([o[i] for o in outs], axis=0)


def setup_inputs(seed: int = 0) -> dict:
    key = jax.random.key(seed)
    k = jax.random.split(key, 32)

    def nrm(i, shape, scale=1.0):
        return scale * jax.random.normal(k[i], shape, F32)

    n_pages = PAST_LEN // PAGE_SIZE
    n_phys = (DEC_BATCH * n_pages * 5) // 4
    win = min(NSA_WINDOW, PAST_LEN)
    bs = LRU_WIDTH // LRU_BLOCKS
    page_table = jax.random.permutation(k[0], n_phys)[:DEC_BATCH * n_pages].reshape(DEC_BATCH, n_pages).astype(jnp.int32)
    a0 = jax.random.uniform(k[1], (DEPTH, LRU_WIDTH), F32, 0.9, 0.999) ** (1.0 / LRU_C)
    return {
        'x_prompt': nrm(2, (BATCH, SEQ, D_MODEL)),
        'mem_prompt': nrm(3, (BATCH, MEM_LEN, D_MODEL)),
        'x_sample': nrm(4, (DEC_BATCH, DEC_SEQ, D_MODEL)),
        'cache_fox_kv': nrm(5, (DEPTH, n_phys, PAGE_SIZE, 2, FOX_HEADS, HEAD_DIM)),
        'cache_fox_logf': jax.nn.log_sigmoid(2.0 + nrm(6, (DEPTH, n_phys, PAGE_SIZE, FOX_HEADS), 0.5)),
        'cache_nsa_kv': nrm(7, (DEPTH, n_phys, PAGE_SIZE, 4, HEAD_DIM)),
        'state_nsa_win': nrm(8, (DEPTH, DEC_BATCH, win, 2, HEAD_DIM)),
        'state_hgrn': nrm(9, (DEPTH, DEC_BATCH, HGRN_HEADS, HEAD_DIM, HEAD_DIM), 0.5),
        'state_lru_h': nrm(10, (DEPTH, DEC_BATCH, LRU_WIDTH), 0.5),
        'state_lru_conv': nrm(11, (DEPTH, DEC_BATCH, LRU_CONV - 1, LRU_WIDTH)),
        'cache_xa_kv': nrm(12, (DEPTH, DEC_BATCH, MEM_LEN, 2, XA_HEADS, XA_HEAD_DIM)),
        'page_table': page_table,
        'w_in': nrm(13, (DEPTH, D_MODEL, D_PROJ), D_MODEL ** -0.5),
        'b_fox_f': 2.0 + nrm(14, (DEPTH, FOX_HEADS), 0.5),
        'hgrn_lower': nrm(15, (DEPTH, HGRN_HEADS * HEAD_DIM), 0.5),
        'nsa_cmp_w': nrm(16, (DEPTH, 2, NSA_CMP_BLOCK, HEAD_DIM), NSA_CMP_BLOCK ** -0.5),
        'lru_conv_w': nrm(17, (DEPTH, LRU_CONV, LRU_WIDTH), LRU_CONV ** -0.5),
        'lru_conv_b': nrm(18, (DEPTH, LRU_WIDTH), 0.01),
        'lru_wa': nrm(19, (DEPTH, LRU_BLOCKS, bs, bs), bs ** -0.5),
        'lru_ba': nrm(20, (DEPTH, LRU_WIDTH), 0.1),
        'lru_wx': nrm(21, (DEPTH, LRU_BLOCKS, bs, bs), bs ** -0.5),
        'lru_bx': nrm(22, (DEPTH, LRU_WIDTH), 0.1),
        'lru_lambda': jnp.log(a0) - jnp.log1p(-a0),
        'g_mix': 1.0 + nrm(23, (DEPTH, D_MIX), 0.05),
        'w_out': nrm(24, (DEPTH, D_MIX, D_MODEL), D_MIX ** -0.5),
        'g_mem': 1.0 + nrm(25, (DEPTH, D_MODEL), 0.05),
        'xa_wq': nrm(26, (DEPTH, D_MODEL, XA_HEADS * XA_HEAD_DIM), D_MODEL ** -0.5),
        'xa_wkv': nrm(27, (DEPTH, D_MODEL, 2 * XA_HEADS * XA_HEAD_DIM), D_MODEL ** -0.5),
        'xa_wo': nrm(28, (DEPTH, XA_HEADS * XA_HEAD_DIM, D_MODEL), (XA_HEADS * XA_HEAD_DIM) ** -0.5),
        'w_ff1': nrm(29, (DEPTH, D_MODEL, D_FF), D_MODEL ** -0.5),
        'w_ff2': nrm(30, (DEPTH, D_FF, D_MODEL), D_FF ** -0.5),
        'norms': 1.0 + nrm(31, (DEPTH, N_NORMS, D_MODEL), 0.05),
    }


def reference(x_prompt, mem_prompt, x_sample, cache_fox_kv, cache_fox_logf, cache_nsa_kv,
              state_nsa_win, state_hgrn, state_lru_h, state_lru_conv, cache_xa_kv, page_table,
              w_in, b_fox_f, hgrn_lower, nsa_cmp_w, lru_conv_w, lru_conv_b, lru_wa, lru_ba,
              lru_wx, lru_bx, lru_lambda, g_mix, w_out, g_mem, xa_wq, xa_wkv, xa_wo,
              w_ff1, w_ff2, norms):
    W = dict(w_in=w_in, b_fox_f=b_fox_f, nsa_cmp_w=nsa_cmp_w, lru_conv_w=lru_conv_w,
             lru_conv_b=lru_conv_b, lru_wa=lru_wa, lru_ba=lru_ba, lru_wx=lru_wx, lru_bx=lru_bx,
             lru_lambda=lru_lambda, g_mix=g_mix, w_out=w_out, g_mem=g_mem, xa_wq=xa_wq,
             xa_wkv=xa_wkv, xa_wo=xa_wo, w_ff1=w_ff1, w_ff2=w_ff2, norms=norms)
    lower = jax.nn.softmax(hgrn_lower.astype(F32), axis=0)
    lbs = jnp.cumsum(lower, axis=0) - lower[0]
    y_p, y_s = x_prompt, x_sample
    p_out, s_out = [], []
    for l in range(DEPTH):
        y_p, po = _prompt_layer(y_p, mem_prompt, W, l, lbs[l])
        y_s, so = _sample_layer(y_s, W, l, lbs[l], page_table, cache_fox_kv[l], cache_fox_logf[l],
                                cache_nsa_kv[l], state_nsa_win[l], state_hgrn[l], state_lru_h[l],
                                state_lru_conv[l], cache_xa_kv[l])
        p_out.append(po)
        s_out.append(so)
    return (y_p, y_s,
            _stack(p_out, 0), _stack(p_out, 1), _stack(p_out, 2), _stack(p_out, 3),
            _stack(p_out, 4), _stack(p_out, 5), _stack(p_out, 6), _stack(p_out, 7),
            _stack(s_out, 0), _stack(s_out, 1), _stack(s_out, 2), _stack(s_out, 3),
            _stack(s_out, 4), _stack(s_out, 5), _stack(s_out, 6))
```

```python
import functools

import jax
import jax.numpy as jnp
import numpy as np
from jax import lax
from jax.experimental import pallas as pl
from jax.experimental.pallas import tpu as pltpu

F32 = jnp.float32
BF16 = jnp.bfloat16
I32 = jnp.int32

D_MODEL = 1024
GROUP_W = 256
HEAD_DIM = 64
N_HEADS = 4
NSA_CMP_BLOCK = 32
NSA_SEL_BLOCK = 64
NSA_TOPK = 16
NSA_WINDOW = 512
LRU_CONV = 4
LRU_C = 8.0
D_FF = 4096
EPS = 1e-6
NEG_BIG = -1e30
LB_FLOOR = 1e-30
SCALE = HEAD_DIM ** -0.5
PAGE = 128

V7X_VMEM_BYTES = 64 * 1024 * 1024
VMEM_LIMIT = V7X_VMEM_BYTES - 8 * 1024 * 1024
LANES = 128

C_FQ, C_FK, C_FV = 0, 256, 512
C_HQ, C_HF, C_HI, C_HG = 768, 1024, 1280, 1536
C_NQ = 1792
C_UX, C_UG = 2048, 2304
C_NKV = 2560
C_FF, C_NG = 2944, 2948
D_PROJ_PAD = 3072


def _params(*sem):
    return pltpu.CompilerParams(dimension_semantics=sem, vmem_limit_bytes=VMEM_LIMIT)


def _const_spec(shape):
    n = len(shape)
    return pl.BlockSpec(shape, lambda *_: (0,) * n)


def _rms(x, g):
    return x * lax.rsqrt(jnp.mean(x * x, axis=-1, keepdims=True) + EPS) * g


def _sigmoid(x):
    return 1.0 / (1.0 + jnp.exp(-x))


def _softplus(x):
    return jnp.maximum(x, 0.0) + jnp.log1p(jnp.exp(-jnp.abs(x)))


def _log_sigmoid(x):
    return -_softplus(-x)


def _gelu_tanh(x):
    return 0.5 * x * (1.0 + jnp.tanh(np.sqrt(2.0 / np.pi) * (x + 0.044715 * (x * x * x))))


def _trunc_bf16(x):
    bits = lax.bitcast_convert_type(x, jnp.uint32) & jnp.uint32(0xFFFF0000)
    return lax.bitcast_convert_type(bits, F32)


def _split3(x):
    hi = _trunc_bf16(x)
    r = x - hi
    mid = _trunc_bf16(r)
    lo = _trunc_bf16(r - mid)
    return hi.astype(BF16), mid.astype(BF16), lo.astype(BF16)


def _dot(a, b):
    return jnp.dot(a, b, preferred_element_type=F32)


def _dot_nt(a, b):
    return lax.dot_general(a, b, (((1,), (1,)), ((), ())), preferred_element_type=F32)


def _dot_tn(a, b):
    return lax.dot_general(a, b, (((0,), (0,)), ((), ())), preferred_element_type=F32)


def _dot_exact_lhs(x, m01):
    hi, mid, lo = _split3(x)
    return _dot(hi, m01) + _dot(mid, m01) + _dot(lo, m01)


def _dot_exact_rhs(m01, x):
    hi, mid, lo = _split3(x)
    return _dot(m01, hi) + _dot(m01, mid) + _dot(m01, lo)


def _rms_matmul_kernel(x_ref, g_ref, w_ref, o_ref):
    h = _rms(x_ref[...], g_ref[...])
    o_ref[...] = _dot(h.astype(BF16), w_ref[...])


def rms_matmul(x, g, w, tm):
    m, k = x.shape
    n = w.shape[1]
    return pl.pallas_call(
        _rms_matmul_kernel,
        grid=(m // tm,),
        in_specs=[pl.BlockSpec((tm, k), lambda i: (i, 0)), _const_spec((1, k)), _const_spec((k, n))],
        out_specs=pl.BlockSpec((tm, n), lambda i: (i, 0)),
        out_shape=jax.ShapeDtypeStruct((m, n), F32),
        compiler_params=_params("parallel"),
        name="rms_matmul",
    )(x, g.reshape(1, k), w)


def _logf_kernel(b_ref, x_ref, u_ref, ls_ref, logf_ref, c_ref, *, valid, chain):
    h = pl.program_id(0)
    lf = _log_sigmoid(x_ref[0] + b_ref[h])
    n = lf.shape[0]
    if valid < LANES:
        lane = lax.broadcasted_iota(I32, (n, LANES), 1)
        lf = jnp.where(lane < valid, lf, 0.0)
    logf_ref[0] = lf
    c = _dot_exact_lhs(lf, u_ref[...])
    if chain:
        tot = jnp.broadcast_to(c[:, LANES - 1:LANES], (n, LANES))
        c = c + _dot_exact_rhs(ls_ref[...], tot)
    c_ref[0] = c


def logf_cumsum(ff, bias, valid, chain):
    nh, n, _ = ff.shape
    u = jnp.triu(jnp.ones((LANES, LANES), F32)).astype(BF16)
    ls = jnp.tril(jnp.ones((n, n), F32), -1).astype(BF16)
    blk = pl.BlockSpec((1, n, LANES), lambda h: (h, 0, 0))
    return pl.pallas_call(
        functools.partial(_logf_kernel, valid=valid, chain=chain),
        grid=(nh,),
        in_specs=[pl.BlockSpec(memory_space=pltpu.SMEM), blk, _const_spec((LANES, LANES)), _const_spec((n, n))],
        out_specs=[blk, blk],
        out_shape=[jax.ShapeDtypeStruct(ff.shape, F32)] * 2,
        compiler_params=_params("parallel"),
        name="logf_cumsum",
    )(bias, ff, u, ls)


def _flash_kernel(q_ref, k_ref, v_ref, o_ref, m_sc, acc_sc, *, tq, tk, hstack, window):
    i = pl.program_id(1)
    rows = hstack * tq
    q = q_ref[0]
    m_sc[...] = jnp.full((rows, 1), NEG_BIG, F32)
    acc_sc[...] = jnp.zeros((rows, LANES), F32)
    qs = i * tq
    jd = qs // tk
    qpos = qs + lax.broadcasted_iota(I32, (rows, 1), 0) % tq

    def step(j, masked):
        off = pl.multiple_of(j * tk, tk)
        kt = k_ref[0, pl.ds(off, tk), :]
        vt = v_ref[0, pl.ds(off, tk), :]
        s = _dot_nt(q, kt)
        if masked:
            kpos = off + lax.broadcasted_iota(I32, (1, tk), 1)
            valid = kpos <= qpos
            if window is not None:
                valid = jnp.logical_and(valid, qpos - kpos < window)
            s = jnp.where(valid, s, NEG_BIG)
        m_old = m_sc[...]
        m_new = jnp.maximum(m_old, jnp.max(s, axis=-1, keepdims=True))
        p = jnp.exp(s - m_new)
        if masked:
            p = jnp.where(valid, p, 0.0)
        alpha = jnp.exp(m_old - m_new)
        acc_sc[...] = alpha * acc_sc[...] + _dot(p.astype(BF16), vt)
        m_sc[...] = m_new

    def loop(lo, hi, masked):
        def body(j, carry):
            step(j, masked)
            return carry
        lax.fori_loop(lo, hi, body, 0)

    if window is None:
        loop(0, jd, False)
        step(jd, True)
    else:
        loop(jnp.maximum(jd - window // tk, 0), jd + 1, True)
    acc = acc_sc[...]
    den = acc[:, HEAD_DIM:HEAD_DIM + 1]
    o_ref[0] = acc / jnp.where(den > 0.0, den, 1.0)


def flash_attention(q, k, v, *, tq, tk, hstack, window=None):
    g, qrows, kd = q.shape
    gk, lk, _ = k.shape
    rows = hstack * tq
    kmap = (lambda a, i: (a, 0, 0)) if gk > 1 else (lambda a, i: (0, 0, 0))
    return pl.pallas_call(
        functools.partial(_flash_kernel, tq=tq, tk=tk, hstack=hstack, window=window),
        grid=(g, qrows // rows),
        in_specs=[pl.BlockSpec((1, rows, kd), lambda a, i: (a, i, 0)),
                  pl.BlockSpec((1, lk, kd), kmap),
                  pl.BlockSpec((1, lk, LANES), kmap)],
        out_specs=pl.BlockSpec((1, rows, LANES), lambda a, i: (a, i, 0)),
        out_shape=jax.ShapeDtypeStruct((g, qrows, LANES), F32),
        scratch_shapes=[pltpu.VMEM((rows, 1), F32), pltpu.VMEM((rows, LANES), F32)],
        compiler_params=_params("parallel", "arbitrary"),
        name="flash_attention",
    )(q, k, v)


def _compress_kernel(kv_ref, w_ref, o_ref):
    kv = kv_ref[...]
    n = kv.shape[0] // NSA_CMP_BLOCK
    blocks = kv.reshape(n, NSA_CMP_BLOCK, LANES) * w_ref[...][None]
    o_ref[...] = jnp.sum(blocks, axis=1)


def nsa_compress(z, w2, tl):
    length = z.shape[0]
    return pl.pallas_call(
        _compress_kernel,
        grid=(length // tl,),
        in_specs=[pl.BlockSpec((tl, LANES), lambda i: (i, C_NKV // LANES)), _const_spec((NSA_CMP_BLOCK, LANES))],
        out_specs=pl.BlockSpec((tl // NSA_CMP_BLOCK, LANES), lambda i: (i, 0)),
        out_shape=jax.ShapeDtypeStruct((length // NSA_CMP_BLOCK, LANES), F32),
        compiler_params=_params("parallel"),
        name="nsa_compress",
    )(z, w2)


def _cmp_topk_kernel(q_ref, kv_ref, oc_ref, sel_ref, idx_ref, *, tq, pos0, n_rounds):
    i = pl.program_id(1)
    q = q_ref[0]
    kv = kv_ref[0]
    n = kv.shape[0]
    nsc = n // 2
    kc = kv[:, :HEAD_DIM].astype(BF16)
    vc = kv[:, HEAD_DIM:].astype(BF16)
    pos = pos0 + i * tq + lax.broadcasted_iota(I32, (tq, 1), 0)
    slot = lax.broadcasted_iota(I32, (1, n), 1)
    cmp_idx = jnp.where(slot < nsc, 2 * slot, 2 * (slot - nsc) + 1)
    valid = (cmp_idx + 1) * NSA_CMP_BLOCK - 1 <= pos
    imp = jnp.zeros((tq, n), F32)
    outs = []
    for h in range(N_HEADS):
        qh = q[:, h * HEAD_DIM:(h + 1) * HEAD_DIM].astype(BF16)
        s = jnp.where(valid, _dot_nt(qh, kc) * SCALE, NEG_BIG)
        m = jnp.max(s, axis=-1, keepdims=True)
        e = jnp.where(valid, jnp.exp(s - m), 0.0)
        d = jnp.sum(e, axis=-1, keepdims=True)
        p = e / jnp.where(d > 0.0, d, 1.0)
        outs.append(_dot(p.astype(BF16), vc))
        imp = imp + p
    oc_ref[0] = jnp.concatenate(outs, axis=-1)

    imp = imp[:, :nsc] + imp[:, nsc:]
    blk = lax.broadcasted_iota(I32, (1, nsc), 1)
    blk_f = blk.astype(F32)
    forced = jnp.logical_or(blk == pos // NSA_SEL_BLOCK, blk == 0)
    causal = blk * NSA_SEL_BLOCK <= pos
    imp = jnp.where(forced, 1e9, jnp.where(causal, imp, -1e9))
    sel = jnp.zeros((tq, nsc), F32)
    lane = lax.broadcasted_iota(I32, (tq, LANES), 1)
    idx = jnp.full((tq, LANES), -1, I32)
    for r in range(n_rounds):
        mx = jnp.max(imp, axis=-1, keepdims=True)
        first = jnp.min(jnp.where(imp == mx, blk_f, float(nsc)), axis=-1, keepdims=True)
        hit = blk_f == first
        ok = mx >= 0.0
        sel = jnp.where(jnp.logical_and(hit, ok), 1.0, sel)
        idx = jnp.where(lane == r, jnp.where(ok, first.astype(I32), -1), idx)
        imp = jnp.where(hit, -3e38, imp)
    sel_ref[0] = jnp.where(sel > 0.0, 0.0, NEG_BIG).astype(BF16)
    idx_ref[0] = idx


def cmp_topk(q, kvcmp, *, tq, pos0, n_rounds):
    g, lq, _ = q.shape
    gk, n, _ = kvcmp.shape
    nsc = n // 2
    kmap = (lambda a, i: (a, 0, 0)) if gk > 1 else (lambda a, i: (0, 0, 0))
    qmap = lambda a, i: (a, i, 0)
    return pl.pallas_call(
        functools.partial(_cmp_topk_kernel, tq=tq, pos0=pos0, n_rounds=n_rounds),
        grid=(g, lq // tq),
        in_specs=[pl.BlockSpec((1, tq, GROUP_W), qmap), pl.BlockSpec((1, n, LANES), kmap)],
        out_specs=[pl.BlockSpec((1, tq, GROUP_W), qmap), pl.BlockSpec((1, tq, nsc), qmap),
                   pl.BlockSpec((1, tq, LANES), qmap)],
        out_shape=[jax.ShapeDtypeStruct((g, lq, GROUP_W), F32), jax.ShapeDtypeStruct((g, lq, nsc), BF16),
                   jax.ShapeDtypeStruct((g, lq, LANES), I32)],
        compiler_params=_params("parallel", "parallel"),
        name="cmp_topk",
    )(q, kvcmp)


HG_SUB = 16
HG_PAD = 16


def _hgrn_kernel(q_ref, f_ref, i_ref, g_ref, low_ref, gain_ref, s0_ref, tril_ref,
                 o_ref, s_out_ref, st_sc, ksh_sc, bsh_sc, vsh_sc, *, layer, chunk, n_chunks, valid_len):
    step = pl.program_id(1)
    sub = min(HG_SUB, chunk)

    @pl.when(step == 0)
    def _():
        st_sc[...] = s0_ref[0]
        zeros = jnp.zeros((HG_PAD, HEAD_DIM), F32)
        ksh_sc[0:HG_PAD, :] = zeros
        bsh_sc[0:HG_PAD, :] = zeros
        vsh_sc[0:HG_PAD, :] = zeros

    low = low_ref[...]
    e = jnp.exp(low - jnp.max(low, axis=0, keepdims=True))
    lower = e / jnp.sum(e, axis=0, keepdims=True)
    cs = lower[0:1]
    for d in range(1, layer + 1):
        cs = cs + lower[d:d + 1]
    lb_all = cs - lower[0:1]
    tril = tril_ref[...]
    row = lax.broadcasted_iota(I32, (chunk, 1), 0)

    def do_chunk(c, carry):
        r0 = pl.multiple_of(c * chunk, chunk)
        base = step * (n_chunks * chunk) + r0
        live = (base + row) < valid_len
        outs = []
        for h in range(N_HEADS):
            cols = slice(h * HEAD_DIM, (h + 1) * HEAD_DIM)
            lb = lb_all[:, cols]
            zq = q_ref[0, pl.ds(r0, chunk), cols]
            zf = f_ref[0, pl.ds(r0, chunk), cols]
            v = i_ref[0, pl.ds(r0, chunk), cols]
            q = zq * _sigmoid(zq)
            a = jnp.log1p(-lb) + _log_sigmoid(zf)
            b0 = jnp.log(jnp.maximum(lb, LB_FLOOR))
            log_f = jnp.maximum(a, b0) + jnp.log1p(jnp.exp(-jnp.abs(a - b0)))
            k = (1.0 - lb) * _sigmoid(-zf)
            log_f = jnp.where(live, log_f, 0.0)
            k = jnp.where(live, k, 0.0)
            b = _dot_exact_rhs(tril, log_f)
            ksh_sc[HG_PAD:HG_PAD + chunk, :] = k
            bsh_sc[HG_PAD:HG_PAD + chunk, :] = b
            vsh_sc[HG_PAD:HG_PAD + chunk, :] = v
            st = st_sc[h]
            o = _dot_nt((q * jnp.exp(b)).astype(BF16), st.astype(BF16))
            for d in range(sub):
                kd = ksh_sc[HG_PAD - d:HG_PAD - d + chunk, :]
                bd = bsh_sc[HG_PAD - d:HG_PAD - d + chunk, :]
                vd = vsh_sc[HG_PAD - d:HG_PAD - d + chunk, :]
                ok = (row % sub) >= d
                w = q * kd * jnp.exp(jnp.where(ok, b - bd, 0.0))
                a_d = jnp.where(ok, jnp.sum(w, axis=-1, keepdims=True), 0.0)
                o = o + a_d * vd
            parts = [o[0:sub]]
            for s_i in range(1, chunk // sub):
                lo = s_i * sub
                b_start = b[lo - 1:lo]
                qd = q[lo:lo + sub] * jnp.exp(b[lo:lo + sub] - b_start)
                kdec = k[0:lo] * jnp.exp(b_start - b[0:lo])
                att = _dot_nt(qd.astype(BF16), kdec.astype(BF16))
                parts.append(o[lo:lo + sub] + _dot(att.astype(BF16), v[0:lo].astype(BF16)))
            o = jnp.concatenate(parts, axis=0) if len(parts) > 1 else parts[0]
            b_last = b[chunk - 1:chunk]
            kl = k * jnp.exp(b_last - b)
            st_sc[h] = jnp.exp(b_last) * st + _dot_tn(v.astype(BF16), kl.astype(BF16))
            outs.append(_rms(o, gain_ref[:, cols]))
        zg = g_ref[0, pl.ds(r0, chunk), :]
        o_ref[0, pl.ds(r0, chunk), :] = jnp.concatenate(outs, axis=-1) * (zg * _sigmoid(zg))
        return carry

    lax.fori_loop(0, n_chunks, do_chunk, 0)

    @pl.when(step == pl.num_programs(1) - 1)
    def _():
        s_out_ref[0] = st_sc[...]


def hgrn(z, hgrn_lower, gain, s0t, *, layer, chunk, n_chunks, valid_len):
    bsz, lp, _ = z.shape
    tb = chunk * n_chunks
    tril = jnp.tril(jnp.ones((chunk, chunk), F32)).astype(BF16)
    col = lambda c: pl.BlockSpec((1, tb, GROUP_W), lambda b, i: (b, i, c // GROUP_W))
    st_spec = pl.BlockSpec((1, N_HEADS, HEAD_DIM, HEAD_DIM), lambda b, i: (b, 0, 0, 0))
    depth = hgrn_lower.shape[0]
    return pl.pallas_call(
        functools.partial(_hgrn_kernel, layer=layer, chunk=chunk, n_chunks=n_chunks, valid_len=valid_len),
        grid=(bsz, lp // tb),
        in_specs=[col(C_HQ), col(C_HF), col(C_HI), col(C_HG), _const_spec((depth, GROUP_W)),
                  _const_spec((1, GROUP_W)), st_spec, _const_spec((chunk, chunk))],
        out_specs=[pl.BlockSpec((1, tb, GROUP_W), lambda b, i: (b, i, 0)), st_spec],
        out_shape=[jax.ShapeDtypeStruct((bsz, lp, GROUP_W), F32),
                   jax.ShapeDtypeStruct((bsz, N_HEADS, HEAD_DIM, HEAD_DIM), F32)],
        scratch_shapes=[pltpu.VMEM((N_HEADS, HEAD_DIM, HEAD_DIM), F32)]
                       + [pltpu.VMEM((HG_PAD + chunk, HEAD_DIM), F32)] * 3,
        compiler_params=_params("parallel", "arbitrary"),
        name="hgrn2",
    )(z, z, z, z, hgrn_lower, gain.reshape(1, GROUP_W), s0t, tril)


LRU_PAD = 8


def _rglru_kernel(ux_ref, ug_ref, conv0_ref, h0_ref, cw_ref, cb_ref, wa_ref, ba_ref, wx_ref, bx_ref,
                  lam_ref, y_ref, hl_ref, xp_sc, a_sc, b_sc, h_sc, hc_sc, *, tb):
    step = pl.program_id(1)

    @pl.when(step == 0)
    def _():
        xp_sc[0:LRU_PAD, :] = conv0_ref[0]
        hc_sc[...] = h0_ref[0]

    x = ux_ref[0]
    xp_sc[LRU_PAD:LRU_PAD + tb, :] = x
    cw = cw_ref[...]
    u = cb_ref[...] + xp_sc[LRU_PAD - 3:LRU_PAD - 3 + tb, :] * cw[0:1]
    for j in range(1, LRU_CONV):
        u = u + xp_sc[LRU_PAD - 3 + j:LRU_PAD - 3 + j + tb, :] * cw[j:j + 1]
    if tb >= LRU_PAD:
        xp_sc[0:LRU_PAD, :] = xp_sc[tb:tb + LRU_PAD, :]
    ub = u.astype(BF16)
    r = _sigmoid(_dot(ub, wa_ref[...]) + ba_ref[...])
    gi = _sigmoid(_dot(ub, wx_ref[...]) + bx_ref[...])
    log_a = -LRU_C * r * _softplus(-lam_ref[...])
    a = jnp.exp(log_a)
    a_sc[...] = a
    b_sc[...] = jnp.sqrt(1.0 - a * a) * (gi * u)

    def body(t, h):
        h = a_sc[pl.ds(t, 1), :] * h + b_sc[pl.ds(t, 1), :]
        h_sc[pl.ds(t, 1), :] = h
        return h

    h = lax.fori_loop(0, tb, body, hc_sc[...], unroll=min(tb, 8))
    hc_sc[...] = h
    y_ref[0] = h_sc[...] * _gelu_tanh(ug_ref[0])
    hl_ref[0] = h


def rglru(z, conv0, h0, cw, cb, wa_bd, ba, wx_bd, bx, lam, *, tb):
    bsz, length, _ = z.shape
    col = lambda c: pl.BlockSpec((1, tb, GROUP_W), lambda b, i: (b, i, c // GROUP_W))
    per_b = lambda r: pl.BlockSpec((1, r, GROUP_W), lambda b, i: (b, 0, 0))
    vec = _const_spec((1, GROUP_W))
    mat = _const_spec((GROUP_W, GROUP_W))
    return pl.pallas_call(
        functools.partial(_rglru_kernel, tb=tb),
        grid=(bsz, length // tb),
        in_specs=[col(C_UX), col(C_UG), per_b(LRU_PAD), per_b(1), _const_spec((LRU_CONV, GROUP_W)), vec,
                  mat, vec, mat, vec, vec],
        out_specs=[pl.BlockSpec((1, tb, GROUP_W), lambda b, i: (b, i, 0)), per_b(1)],
        out_shape=[jax.ShapeDtypeStruct((bsz, length, GROUP_W), F32), jax.ShapeDtypeStruct((bsz, 1, GROUP_W), F32)],
        scratch_shapes=[pltpu.VMEM((LRU_PAD + tb + LRU_PAD, GROUP_W), F32), pltpu.VMEM((tb, GROUP_W), F32),
                        pltpu.VMEM((tb, GROUP_W), F32), pltpu.VMEM((tb, GROUP_W), F32),
                        pltpu.VMEM((1, GROUP_W), F32)],
        compiler_params=_params("parallel", "arbitrary"),
        name="rglru",
    )(z, z, conv0, h0, cw, cb.reshape(1, -1), wa_bd, ba.reshape(1, -1), wx_bd, bx.reshape(1, -1),
      lam.reshape(1, -1))


def _mix_kernel(x_ref, oa_ref, ob_ref, oc_ref, os_ref, ow_ref, od_ref, zl_ref, gm_ref, w_ref, n_ref, y_ref):
    gm = gm_ref[...]
    gates = _sigmoid(zl_ref[...][:, C_NG - C_FF:C_NG - C_FF + 3 * N_HEADS])
    oc, osel, ow = oc_ref[...], os_ref[...], ow_ref[...]
    parts = []
    for h in range(N_HEADS):
        cols = slice(h * HEAD_DIM, (h + 1) * HEAD_DIM)
        parts.append(gates[:, 3 * h:3 * h + 1] * oc[:, cols] + gates[:, 3 * h + 1:3 * h + 2] * osel[:, cols]
                     + gates[:, 3 * h + 2:3 * h + 3] * ow[:, cols])
    o_nsa = jnp.concatenate(parts, axis=-1)
    groups = (_rms(oa_ref[...], gm[:, 0:GROUP_W]), ob_ref[...],
              _rms(o_nsa, gm[:, 2 * GROUP_W:3 * GROUP_W]), _rms(od_ref[...], gm[:, 3 * GROUP_W:]))
    y = jnp.zeros(x_ref.shape, F32)
    for gi, grp in enumerate(groups):
        y = y + _dot(grp.astype(BF16), w_ref[gi * GROUP_W:(gi + 1) * GROUP_W, :])
    y_ref[...] = x_ref[...] + _rms(y, n_ref[...])


def mix(x, oa, ob, oc, osel, ow, od, z, gm, w_out, n1, tm):
    m = x.shape[0]
    row = lambda w: pl.BlockSpec((tm, w), lambda i: (i, 0))
    return pl.pallas_call(
        _mix_kernel,
        grid=(m // tm,),
        in_specs=[row(D_MODEL)] + [row(GROUP_W)] * 6
                 + [pl.BlockSpec((tm, LANES), lambda i: (i, C_FF // LANES)), _const_spec((1, D_MODEL)),
                    _const_spec((D_MODEL, D_MODEL)), _const_spec((1, D_MODEL))],
        out_specs=row(D_MODEL),
        out_shape=jax.ShapeDtypeStruct((m, D_MODEL), F32),
        compiler_params=_params("parallel"),
        name="mix_residual",
    )(x, oa, ob, oc, osel, ow, od, z, gm.reshape(1, -1), w_out, n1.reshape(1, -1))


def _xattn_kernel(x_ref, n2_ref, wq_ref, kv_ref, wo_ref, n3_ref, y_ref):
    x = x_ref[0]
    q = _dot(_rms(x, n2_ref[...]).astype(BF16), wq_ref[...])
    kv = kv_ref[0].astype(BF16)
    hw = N_HEADS * HEAD_DIM
    o = jnp.zeros(x.shape, F32)
    for h in range(N_HEADS):
        cols = slice(h * HEAD_DIM, (h + 1) * HEAD_DIM)
        s = _dot_nt(q[:, cols].astype(BF16), kv[:, cols]) * SCALE
        e = jnp.exp(s - jnp.max(s, axis=-1, keepdims=True))
        p = e / jnp.sum(e, axis=-1, keepdims=True)
        oh = _dot(p.astype(BF16), kv[:, hw + h * HEAD_DIM:hw + (h + 1) * HEAD_DIM])
        o = o + _dot(oh.astype(BF16), wo_ref[cols, :])
    y_ref[0] = x + _rms(o, n3_ref[...])


def xattn(x, n2, wq, kv, wo, n3, tm):
    g, m, _ = x.shape
    gk, mem, kvw = kv.shape
    kmap = (lambda a, i: (a, 0, 0)) if gk > 1 else (lambda a, i: (0, 0, 0))
    xs = pl.BlockSpec((1, tm, D_MODEL), lambda a, i: (a, i, 0))
    return pl.pallas_call(
        _xattn_kernel,
        grid=(g, m // tm),
        in_specs=[xs, _const_spec((1, D_MODEL)), _const_spec((D_MODEL, GROUP_W)), pl.BlockSpec((1, mem, kvw), kmap),
                  _const_spec((GROUP_W, D_MODEL)), _const_spec((1, D_MODEL))],
        out_specs=xs,
        out_shape=jax.ShapeDtypeStruct(x.shape, F32),
        compiler_params=_params("parallel", "parallel"),
        name="cross_attention",
    )(x, n2.reshape(1, -1), wq, kv, wo, n3.reshape(1, -1))


FF_CHUNK = 1024


def _ffn_kernel(x_ref, n4_ref, w1_ref, w2_ref, n5_ref, y_ref):
    x = x_ref[...]
    h = _rms(x, n4_ref[...]).astype(BF16)
    f = jnp.zeros(x.shape, F32)
    for c in range(D_FF // FF_CHUNK):
        cols = slice(c * FF_CHUNK, (c + 1) * FF_CHUNK)
        a = jnp.maximum(_dot(h, w1_ref[:, cols]), 0.0)
        f = f + _dot((a * a).astype(BF16), w2_ref[cols, :])
    y_ref[...] = x + _rms(f, n5_ref[...])


def ffn(x, n4, w1, w2, n5, tm):
    m = x.shape[0]
    row = pl.BlockSpec((tm, D_MODEL), lambda i: (i, 0))
    once = pl.Buffered(1)
    return pl.pallas_call(
        _ffn_kernel,
        grid=(m // tm,),
        in_specs=[row, _const_spec((1, D_MODEL)),
                  pl.BlockSpec((D_MODEL, D_FF), lambda i: (0, 0), pipeline_mode=once),
                  pl.BlockSpec((D_FF, D_MODEL), lambda i: (0, 0), pipeline_mode=once),
                  _const_spec((1, D_MODEL))],
        out_specs=row,
        out_shape=jax.ShapeDtypeStruct((m, D_MODEL), F32),
        compiler_params=_params("parallel"),
        name="ffn",
    )(x, n4.reshape(1, -1), w1, w2, n5.reshape(1, -1))


FOX_PG = 8


def _fox_sample_kernel(pt_ref, q_ref, cnb_ref, cnk_ref, kn_ref, vn_ref, m2_ref, *refs, n_tok, layer):
    kv_refs = refs[:FOX_PG]
    lf_refs = refs[FOX_PG:2 * FOX_PG]
    o_ref = refs[2 * FOX_PG]
    m_sc, acc_sc, carry_sc = refs[2 * FOX_PG + 1:]
    g = pl.program_id(1)
    rows = N_HEADS * n_tok
    q = q_ref[0]
    cnb = cnb_ref[0]

    def update(s, valid, vt):
        m_old = m_sc[...]
        m_new = jnp.maximum(m_old, jnp.max(s, axis=-1, keepdims=True))
        p = jnp.exp(s - m_new)
        if valid is not None:
            p = jnp.where(valid, p, 0.0)
        alpha = jnp.exp(m_old - m_new)
        l_add = jnp.sum(p, axis=-1, keepdims=True)
        acc = acc_sc[...]
        acc_sc[:, 0:GROUP_W] = alpha * acc[:, 0:GROUP_W] + _dot(p.astype(BF16), vt)
        acc_sc[:, GROUP_W:] = alpha * acc[:, GROUP_W:] + l_add
        m_sc[...] = m_new

    @pl.when(g == 0)
    def _():
        m_sc[...] = jnp.full((rows, 1), NEG_BIG, F32)
        acc_sc[...] = jnp.zeros(acc_sc.shape, F32)
        carry_sc[...] = jnp.zeros(carry_sc.shape, F32)
        nk = kn_ref.shape[1]
        t_idx = lax.broadcasted_iota(I32, (rows, nk), 0) % n_tok
        j_idx = lax.broadcasted_iota(I32, (rows, nk), 1)
        valid = jnp.logical_and(j_idx <= t_idx, j_idx < n_tok)
        s = _dot_nt(q, kn_ref[0]) + cnb[:, 0:nk] - cnk_ref[0][:, 0:nk]
        update(jnp.where(valid, s, NEG_BIG), valid, vn_ref[0])

    for i in range(FOX_PG):
        page = kv_refs[i][0, 0]
        sums = _dot_exact_lhs(lf_refs[i][0, 0], m2_ref[...])
        carry = carry_sc[...]
        bias_rows = []
        for h in range(N_HEADS):
            sl = slice(h * PAGE, (h + 1) * PAGE)
            bias_rows.append(jnp.broadcast_to(sums[:, sl] + carry[:, sl], (n_tok, PAGE)))
        bias = jnp.concatenate(bias_rows, axis=0) + cnb
        s = _dot_nt(q, page[:, 0:GROUP_W].astype(BF16)) + bias
        update(s, None, page[:, GROUP_W:].astype(BF16))
        carry_sc[...] = carry + sums[:, N_HEADS * PAGE:]

    @pl.when(g == pl.num_programs(1) - 1)
    def _():
        acc = acc_sc[...]
        o = acc[:, 0:GROUP_W] / acc[:, GROUP_W:GROUP_W + 1]
        r_h = lax.broadcasted_iota(I32, (rows, GROUP_W), 0) // n_tok
        c_h = lax.broadcasted_iota(I32, (rows, GROUP_W), 1) // HEAD_DIM
        o = jnp.where(r_h == c_h, o, 0.0)
        out = o[0:n_tok]
        for h in range(1, N_HEADS):
            out = out + o[h * n_tok:(h + 1) * n_tok]
        o_ref[0] = out


def fox_sample(page_table, qbd, cnb, cnk, knew, vnew, cache_kv, cache_logf, *, layer, n_tok):
    bsz, n_pages = page_table.shape
    rows = N_HEADS * n_tok
    ngrp = n_pages // FOX_PG
    i_idx = np.arange(PAGE * N_HEADS) // N_HEADS
    h_idx = np.arange(PAGE * N_HEADS) % N_HEADS
    col_h = np.arange(PAGE * N_HEADS) // PAGE
    col_j = np.arange(PAGE * N_HEADS) % PAGE
    same = h_idx[:, None] == col_h[None, :]
    m2 = np.concatenate([same & (i_idx[:, None] > col_j[None, :]), same], axis=1).astype(np.float32)
    m2 = jnp.asarray(m2, BF16)

    def page_map(i):
        return lambda b, g, pt: (layer, pt[b, n_pages - 1 - (g * FOX_PG + i)], 0, 0)

    per_b = lambda shape: pl.BlockSpec((1,) + shape, lambda b, g, pt: (b, 0, 0))
    grid_spec = pltpu.PrefetchScalarGridSpec(
        num_scalar_prefetch=1,
        grid=(bsz, ngrp),
        in_specs=[per_b((rows, GROUP_W)), per_b((rows, LANES)), per_b((rows, LANES)), per_b((8, GROUP_W)),
                  per_b((8, GROUP_W)), pl.BlockSpec(m2.shape, lambda b, g, pt: (0, 0))]
                 + [pl.BlockSpec((1, 1, PAGE, 2 * GROUP_W), page_map(i)) for i in range(FOX_PG)]
                 + [pl.BlockSpec((1, 1, 1, PAGE * N_HEADS), page_map(i)) for i in range(FOX_PG)],
        out_specs=pl.BlockSpec((1, n_tok, GROUP_W), lambda b, g, pt: (b, 0, 0)),
        scratch_shapes=[pltpu.VMEM((rows, 1), F32), pltpu.VMEM((rows, GROUP_W + LANES), F32),
                        pltpu.VMEM((1, PAGE * N_HEADS), F32)],
    )
    return pl.pallas_call(
        functools.partial(_fox_sample_kernel, n_tok=n_tok, layer=layer),
        grid_spec=grid_spec,
        out_shape=jax.ShapeDtypeStruct((bsz, n_tok, GROUP_W), F32),
        compiler_params=_params("parallel", "arbitrary"),
        name="fox_sample",
    )(page_table, qbd, cnb, cnk, knew, vnew, m2, *([cache_kv] * FOX_PG), *([cache_logf] * FOX_PG))


CMP_PG = 8


def _cmp_pages_kernel(pt_ref, w_ref, *refs):
    o_ref = refs[CMP_PG]
    per_page = PAGE // NSA_CMP_BLOCK
    w = w_ref[...][None]
    outs = []
    for i in range(CMP_PG):
        page = refs[i][0, 0]
        outs.append(jnp.sum(page.reshape(per_page, NSA_CMP_BLOCK, LANES) * w, axis=1))
    o_ref[0] = jnp.concatenate(outs, axis=0)


def nsa_compress_pages(page_table, w2, cache_nsa, *, layer):
    bsz, n_pages = page_table.shape
    per_step = CMP_PG * PAGE // NSA_CMP_BLOCK

    def page_map(i):
        return lambda b, g, pt: (layer, pt[b, g * CMP_PG + i], 0, 0)

    grid_spec = pltpu.PrefetchScalarGridSpec(
        num_scalar_prefetch=1,
        grid=(bsz, n_pages // CMP_PG),
        in_specs=[pl.BlockSpec((NSA_CMP_BLOCK, LANES), lambda b, g, pt: (0, 0))]
                 + [pl.BlockSpec((1, 1, PAGE, LANES), page_map(i)) for i in range(CMP_PG)],
        out_specs=pl.BlockSpec((1, per_step, LANES), lambda b, g, pt: (b, g, 0)),
    )
    return pl.pallas_call(
        _cmp_pages_kernel,
        grid_spec=grid_spec,
        out_shape=jax.ShapeDtypeStruct((bsz, n_pages * PAGE // NSA_CMP_BLOCK, LANES), F32),
        compiler_params=_params("parallel", "parallel"),
        name="nsa_compress_pages",
    )(page_table, w2, *([cache_nsa] * CMP_PG))


def _slc_sample_kernel(idx_ref, pt_ref, q_ref, kn_ref, vn_ref, *refs, n_sel, n_tok):
    blk_refs = refs[:n_sel]
    o_ref = refs[n_sel]
    t = pl.program_id(0) % n_tok
    q = q_ref[0]
    ks = jnp.concatenate([r[0, 0][:, 0:HEAD_DIM] for r in blk_refs], axis=0).astype(BF16)
    vs = jnp.concatenate([r[0, 0][:, HEAD_DIM:] for r in blk_refs], axis=0).astype(BF16)
    s_old = _dot_nt(q, ks) * SCALE
    nk = kn_ref.shape[1]
    s_new = _dot_nt(q, kn_ref[0]) * SCALE
    ok_new = lax.broadcasted_iota(I32, (q.shape[0], nk), 1) <= t
    s_new = jnp.where(ok_new, s_new, NEG_BIG)
    m = jnp.maximum(jnp.max(s_old, axis=-1, keepdims=True), jnp.max(s_new, axis=-1, keepdims=True))
    e_old = jnp.exp(s_old - m)
    e_new = jnp.where(ok_new, jnp.exp(s_new - m), 0.0)
    d = jnp.sum(e_old, axis=-1, keepdims=True) + jnp.sum(e_new, axis=-1, keepdims=True)
    o = _dot((e_old / d).astype(BF16), vs) + _dot((e_new / d).astype(BF16), vn_ref[0])
    o_ref[0] = o


def slc_sample(idx, page_table, q, knew, vnew, cache_nsa_half, *, layer, n_sel, n_tok):
    n_q = q.shape[0]
    blocks_per_page = PAGE // NSA_SEL_BLOCK

    def blk_map(j):
        def f(i, idx_r, pt):
            blk = idx_r[i, j]
            return (layer, pt[i // n_tok, blk // blocks_per_page] * blocks_per_page + blk % blocks_per_page, 0, 1)
        return f

    per_b = lambda shape: pl.BlockSpec((1,) + shape, lambda i, idx_r, pt: (i // n_tok, 0, 0))
    grid_spec = pltpu.PrefetchScalarGridSpec(
        num_scalar_prefetch=2,
        grid=(n_q,),
        in_specs=[pl.BlockSpec((1, 8, HEAD_DIM), lambda i, idx_r, pt: (i, 0, 0)), per_b((8, HEAD_DIM)),
                  per_b((8, HEAD_DIM))]
                 + [pl.BlockSpec((1, 1, NSA_SEL_BLOCK, LANES), blk_map(j)) for j in range(n_sel)],
        out_specs=pl.BlockSpec((1, 8, HEAD_DIM), lambda i, idx_r, pt: (i, 0, 0)),
    )
    return pl.pallas_call(
        functools.partial(_slc_sample_kernel, n_sel=n_sel, n_tok=n_tok),
        grid_spec=grid_spec,
        out_shape=jax.ShapeDtypeStruct((n_q, 8, HEAD_DIM), F32),
        compiler_params=_params("parallel"),
        name="nsa_slc_sample",
    )(idx, page_table, q, knew, vnew, *([cache_nsa_half] * n_sel))


def _win_sample_kernel(q_ref, k_ref, v_ref, o_ref, *, n_tok, n_win, n_keys):
    q = q_ref[0]
    rows, width = q.shape[0], k_ref.shape[1]
    t = lax.broadcasted_iota(I32, (rows, width), 0) % n_tok
    j = lax.broadcasted_iota(I32, (rows, width), 1)
    dist = n_win + t - j
    valid = jnp.logical_and(jnp.logical_and(dist >= 0, dist < NSA_WINDOW), j < n_keys)
    s = jnp.where(valid, _dot_nt(q, k_ref[0]) * SCALE, NEG_BIG)
    m = jnp.max(s, axis=-1, keepdims=True)
    e = jnp.where(valid, jnp.exp(s - m), 0.0)
    d = jnp.sum(e, axis=-1, keepdims=True)
    o_ref[0] = _dot((e / jnp.where(d > 0.0, d, 1.0)).astype(BF16), v_ref[0])


def win_sample(q, kw, vw, *, n_tok, n_win, n_keys):
    bsz, rows, _ = q.shape
    width = kw.shape[1]
    spec = lambda r: pl.BlockSpec((1, r, HEAD_DIM), lambda b: (b, 0, 0))
    return pl.pallas_call(
        functools.partial(_win_sample_kernel, n_tok=n_tok, n_win=n_win, n_keys=n_keys),
        grid=(bsz,),
        in_specs=[spec(rows), spec(width), spec(width)],
        out_specs=spec(rows),
        out_shape=jax.ShapeDtypeStruct((bsz, rows, HEAD_DIM), F32),
        compiler_params=_params("parallel"),
        name="nsa_win_sample",
    )(q, kw, vw)


def _split3_host(x):
    bits = lambda a: lax.bitcast_convert_type(
        lax.bitcast_convert_type(a, jnp.uint32) & jnp.uint32(0xFFFF0000), F32)
    hi = bits(x)
    r = x - hi
    mid = bits(r)
    lo = bits(r - mid)
    return hi.astype(BF16), mid.astype(BF16), lo.astype(BF16)


def _heads(a):
    return a.reshape(a.shape[0], N_HEADS, HEAD_DIM).transpose(1, 0, 2)


def _pad_lanes(parts, width, dtype):
    used = sum(p.shape[-1] for p in parts)
    lead = parts[0].shape[:-1]
    return jnp.concatenate([p.astype(dtype) for p in parts] + [jnp.zeros(lead + (width - used,), dtype)], axis=-1)


def _stack_heads_tiles(a, tq):
    length, _, w = a.shape
    return a.reshape(length // tq, tq, N_HEADS, w).transpose(0, 2, 1, 3).reshape(-1, w)


def _unstack_heads_tiles(o, tq):
    o = o[:, :HEAD_DIM].reshape(-1, N_HEADS, tq, HEAD_DIM).transpose(0, 2, 1, 3)
    return o.reshape(-1, GROUP_W)


def _evens_odds(a):
    return jnp.concatenate([a[..., 0::2, :], a[..., 1::2, :]], axis=-2)


def _pick(n, candidates):
    for c in candidates:
        if n % c == 0:
            return c
    raise ValueError(f"no tile size for {n}")


def _layer_weights(w, l):
    w_in = w["w_in"][l]
    w_in_p = jnp.concatenate(
        [w_in[:, 0:768], w_in[:, 772:2052], w_in[:, 2448:2960], w_in[:, 2052:2436], w_in[:, 768:772],
         w_in[:, 2436:2448],
         jnp.zeros((D_MODEL, D_PROJ_PAD - 2960), F32)], axis=1).astype(BF16)
    bd = lambda m: jax.scipy.linalg.block_diag(*[m[i] for i in range(m.shape[0])]).astype(BF16)
    return dict(
        w_in=w_in_p, n=w["norms"][l], b_fox=w["b_fox_f"][l],
        w2cmp=jnp.concatenate([w["nsa_cmp_w"][l, 0], w["nsa_cmp_w"][l, 1]], axis=-1),
        cw=w["lru_conv_w"][l], cb=w["lru_conv_b"][l], wa=bd(w["lru_wa"][l]), ba=w["lru_ba"][l],
        wx=bd(w["lru_wx"][l]), bx=w["lru_bx"][l], lam=w["lru_lambda"][l], gm=w["g_mix"][l],
        w_out=w["w_out"][l].astype(BF16), g_mem=w["g_mem"][l], wq=w["xa_wq"][l].astype(BF16),
        wkv=w["xa_wkv"][l].astype(BF16), wo=w["xa_wo"][l].astype(BF16), w1=w["w_ff1"][l].astype(BF16),
        w2=w["w_ff2"][l].astype(BF16))


def _prompt_layer(x, mem, wl, l, hgrn_lower):
    length = x.shape[0]
    tm = _pick(length, (512, 256, 128))
    z = rms_matmul(x, wl["n"][0], wl["w_in"], tm)

    ff = z[:, C_FF:C_FF + N_HEADS].T.reshape(N_HEADS, length // LANES, LANES)
    logf, c = logf_cumsum(ff, wl["b_fox"], valid=LANES, chain=True)
    logf = logf.reshape(N_HEADS, length)
    c = c.reshape(N_HEADS, length)
    c3 = [p[..., None] for p in _split3_host(c)]
    ones3 = jnp.ones((N_HEADS, length, 3), BF16)
    q_a = _pad_lanes([_heads(z[:, C_FQ:C_FQ + GROUP_W]) * SCALE] + c3 + [ones3], LANES, BF16)
    k_a = _pad_lanes([_heads(z[:, C_FK:C_FK + GROUP_W]), ones3] + [-p for p in c3], LANES, BF16)
    one = jnp.ones((N_HEADS, length, 1), BF16)
    v_a = _pad_lanes([_heads(z[:, C_FV:C_FV + GROUP_W]), one], LANES, BF16)
    t_a = _pick(length, (512, 256, 128))
    o_a = flash_attention(q_a, k_a, v_a, tq=t_a, tk=t_a, hstack=1)
    o_a = o_a[:, :, :HEAD_DIM].transpose(1, 0, 2).reshape(length, GROUP_W)

    chunk = 64
    n_chunks = _pick(length // chunk, (8, 4, 2, 1))
    s0t = jnp.zeros((1, N_HEADS, HEAD_DIM, HEAD_DIM), F32)
    o_b, st = hgrn(z[None], hgrn_lower, wl["gm"][GROUP_W:2 * GROUP_W], s0t, layer=l, chunk=chunk,
                   n_chunks=n_chunks, valid_len=length)
    o_b = o_b[0]
    s_fin = st.transpose(0, 1, 3, 2)

    tq = 128
    kvcmp = nsa_compress(z, wl["w2cmp"], _pick(length, (1024, 512, 256, 128)))
    n_cmp = length // NSA_CMP_BLOCK
    ns = length // NSA_SEL_BLOCK
    nq = z[:, C_NQ:C_NQ + GROUP_W]
    o_c, selbias, _ = cmp_topk(nq[None], _evens_odds(kvcmp)[None], tq=tq, pos0=0, n_rounds=min(NSA_TOPK, ns))
    nq4 = nq.reshape(length, N_HEADS, HEAD_DIM) * SCALE
    zeros64 = jnp.zeros((length, N_HEADS, HEAD_DIM), BF16)
    sel4 = jnp.broadcast_to(selbias[0][:, None, :], (length, N_HEADS, ns))
    q_s = _stack_heads_tiles(jnp.concatenate([nq4.astype(BF16), zeros64, sel4], axis=-1), tq)
    onehot = (jnp.arange(length)[:, None] // NSA_SEL_BLOCK == jnp.arange(ns)[None, :]).astype(BF16)
    nkv = z[:, C_NKV:C_NKV + 6 * HEAD_DIM]
    z64 = jnp.zeros((length, HEAD_DIM), BF16)
    one1 = jnp.ones((length, 1), BF16)
    k_s = jnp.concatenate([nkv[:, 128:192].astype(BF16), z64, onehot], axis=-1)
    v_s = _pad_lanes([nkv[:, 192:256], one1], LANES, BF16)
    tk_s = _pick(length, (512, 256, 128))
    o_s = flash_attention(q_s[None], k_s[None], v_s[None], tq=tq, tk=tk_s, hstack=N_HEADS)
    o_s = _unstack_heads_tiles(o_s[0], tq)
    q_w = _stack_heads_tiles(jnp.concatenate([nq4.astype(BF16), zeros64], axis=-1), tq)
    k_w = jnp.concatenate([nkv[:, 256:320].astype(BF16), z64], axis=-1)
    v_w = _pad_lanes([nkv[:, 320:384], one1], LANES, BF16)
    o_w = flash_attention(q_w[None], k_w[None], v_w[None], tq=tq, tk=tq, hstack=N_HEADS, window=NSA_WINDOW)
    o_w = _unstack_heads_tiles(o_w[0], tq)

    conv0 = jnp.zeros((1, LRU_PAD, GROUP_W), F32)
    h0 = jnp.zeros((1, 1, GROUP_W), F32)
    o_d, h_last = rglru(z[None], conv0, h0, wl["cw"], wl["cb"], wl["wa"], wl["ba"], wl["wx"], wl["bx"],
                        wl["lam"], tb=_pick(length, (512, 256, 128)))

    y = mix(x, o_a, o_b, o_c[0], o_s, o_w, o_d[0], z, wl["gm"], wl["w_out"], wl["n"][1], tm)
    xa_kv = rms_matmul(mem, wl["g_mem"], wl["wkv"], mem.shape[0])
    y = xattn(y[None], wl["n"][2], wl["wq"], xa_kv[None], wl["wo"], wl["n"][3], tm)[0]
    y = ffn(y, wl["n"][4], wl["w1"], wl["w2"], wl["n"][5], _pick(length, (256, 128)))

    wb = min(NSA_WINDOW, length)
    state = (
        z[:, C_FK:C_FK + 2 * GROUP_W].reshape(1, length, 2, N_HEADS, HEAD_DIM),
        logf.T[None],
        nkv[:, :4 * HEAD_DIM].reshape(1, length, 4, HEAD_DIM),
        nkv[length - wb:, 4 * HEAD_DIM:].reshape(1, wb, 2, HEAD_DIM),
        s_fin,
        h_last[:, 0],
        z[length - (LRU_CONV - 1):, C_UX:C_UX + GROUP_W][None],
        xa_kv.reshape(1, mem.shape[0], 2, N_HEADS, HEAD_DIM),
    )
    return y, state


def _sample_layer(x, wl, l, hgrn_lower, page_table, cache_fox_kv, cache_fox_logf, cache_nsa_kv, win_buf,
                  s0, h0, conv_buf, xa_kv):
    bsz, n_tok, _ = x.shape
    rows = bsz * n_tok
    n_pages = page_table.shape[1]
    past = n_pages * PAGE
    z = rms_matmul(x.reshape(rows, D_MODEL), wl["n"][0], wl["w_in"], rows)
    zb = z.reshape(bsz, n_tok, D_PROJ_PAD)

    ff = jnp.pad(zb[:, :, C_FF:C_FF + N_HEADS].transpose(2, 0, 1), ((0, 0), (0, 0), (0, LANES - n_tok)))
    logf, cn = logf_cumsum(ff, wl["b_fox"], valid=n_tok, chain=False)
    logf_new = logf[:, :, :n_tok].transpose(1, 2, 0)
    cn_t = cn[:, :, :n_tok].transpose(1, 0, 2)
    cnb = jnp.broadcast_to(cn_t.reshape(bsz, N_HEADS * n_tok, 1), (bsz, N_HEADS * n_tok, LANES))
    cnk = jnp.pad(jnp.repeat(cn_t, n_tok, axis=1), ((0, 0), (0, 0), (0, LANES - n_tok)))
    fq = zb[:, :, C_FQ:C_FQ + GROUP_W] * SCALE
    head_mask = (jnp.arange(N_HEADS)[:, None] == jnp.arange(GROUP_W)[None, :] // HEAD_DIM)
    qbd = jnp.where(head_mask[None, :, None, :], fq[:, None], 0.0).reshape(bsz, N_HEADS * n_tok, GROUP_W)
    pad_tok = ((0, 0), (0, 8 - n_tok), (0, 0))
    knew = jnp.pad(zb[:, :, C_FK:C_FK + GROUP_W], pad_tok).astype(BF16)
    vnew = jnp.pad(zb[:, :, C_FV:C_FV + GROUP_W], pad_tok).astype(BF16)
    depth, n_phys = cache_fox_kv.shape[:2]
    o_a = fox_sample(page_table, qbd.astype(BF16), cnb, cnk, knew, vnew,
                     cache_fox_kv.reshape(depth, n_phys, PAGE, 2 * GROUP_W),
                     cache_fox_logf.reshape(depth, n_phys, 1, PAGE * N_HEADS), layer=l, n_tok=n_tok)

    chunk = 8
    zp = jnp.pad(zb, ((0, 0), (0, chunk - n_tok), (0, 0)))
    o_b, st = hgrn(zp, hgrn_lower, wl["gm"][GROUP_W:2 * GROUP_W], s0.transpose(0, 1, 3, 2), layer=l,
                   chunk=chunk, n_chunks=1, valid_len=n_tok)
    o_b = o_b[:, :n_tok]
    s_fin = st.transpose(0, 1, 3, 2)

    nkv = zb[:, :, C_NKV:C_NKV + 6 * HEAD_DIM]
    nq = zb[:, :, C_NQ:C_NQ + GROUP_W]
    kvcmp = nsa_compress_pages(page_table, wl["w2cmp"], cache_nsa_kv.reshape(depth, n_phys, PAGE, 4 * HEAD_DIM),
                               layer=l)
    n_in = min(NSA_TOPK, past // NSA_SEL_BLOCK + 1) - 1
    o_c, _, idx = cmp_topk(nq, _evens_odds(kvcmp), tq=n_tok, pos0=past, n_rounds=n_in)
    q_rows = jnp.pad(nq.reshape(rows, N_HEADS, HEAD_DIM), ((0, 0), (0, 8 - N_HEADS), (0, 0))).astype(BF16)
    ks_new = jnp.pad(nkv[:, :, 128:192], pad_tok).astype(BF16)
    vs_new = jnp.pad(nkv[:, :, 192:256], pad_tok).astype(BF16)
    o_s = slc_sample(idx.reshape(rows, LANES), page_table, q_rows, ks_new, vs_new,
                     cache_nsa_kv.reshape(depth, n_phys * (PAGE // NSA_SEL_BLOCK), NSA_SEL_BLOCK, 4 * HEAD_DIM),
                     layer=l, n_sel=n_in, n_tok=n_tok)
    o_s = o_s[:, :N_HEADS].reshape(bsz, n_tok, GROUP_W)
    n_win = win_buf.shape[1]
    win = jnp.concatenate([win_buf, nkv[:, :, 256:384].reshape(bsz, n_tok, 2, HEAD_DIM)], axis=1)
    n_keys = n_win + n_tok
    kpad = -(-n_keys // 8) * 8
    win_p = jnp.pad(win, ((0, 0), (0, kpad - n_keys), (0, 0), (0, 0))).astype(BF16)
    q_ht = nq.reshape(bsz, n_tok, N_HEADS, HEAD_DIM).transpose(0, 2, 1, 3).reshape(bsz, N_HEADS * n_tok, HEAD_DIM)
    o_w = win_sample(q_ht.astype(BF16), win_p[:, :, 0], win_p[:, :, 1], n_tok=n_tok, n_win=n_win, n_keys=n_keys)
    o_w = o_w.reshape(bsz, N_HEADS, n_tok, HEAD_DIM).transpose(0, 2, 1, 3).reshape(bsz, n_tok, GROUP_W)

    conv0 = jnp.pad(conv_buf, ((0, 0), (LRU_PAD - (LRU_CONV - 1), 0), (0, 0)))
    o_d, h_last = rglru(zb, conv0, h0[:, None], wl["cw"], wl["cb"], wl["wa"], wl["ba"], wl["wx"], wl["bx"],
                        wl["lam"], tb=n_tok)

    flat = lambda a: a.reshape(rows, GROUP_W)
    y = mix(x.reshape(rows, D_MODEL), flat(o_a), flat(o_b), flat(o_c), flat(o_s), flat(o_w), flat(o_d), z,
            wl["gm"], wl["w_out"], wl["n"][1], rows)
    mem_len = xa_kv.shape[1]
    y = xattn(y.reshape(bsz, n_tok, D_MODEL), wl["n"][2], wl["wq"], xa_kv.reshape(bsz, mem_len, 2 * GROUP_W),
              wl["wo"], wl["n"][3], n_tok)
    y = ffn(y.reshape(rows, D_MODEL), wl["n"][4], wl["w1"], wl["w2"], wl["n"][5], rows)

    state = (
        zb[:, :, C_FK:C_FK + 2 * GROUP_W].reshape(bsz, n_tok, 2, N_HEADS, HEAD_DIM),
        logf_new,
        nkv[:, :, :4 * HEAD_DIM].reshape(bsz, n_tok, 4, HEAD_DIM),
        win[:, n_tok:],
        s_fin,
        h_last[:, 0],
        jnp.concatenate([conv_buf, zb[:, :, C_UX:C_UX + GROUP_W]], axis=1)[:, n_tok:],
    )
    return y.reshape(bsz, n_tok, D_MODEL), state


def kernel(x_prompt, mem_prompt, x_sample, cache_fox_kv, cache_fox_logf, cache_nsa_kv, state_nsa_win, state_hgrn,
           state_lru_h, state_lru_conv, cache_xa_kv, page_table, w_in, b_fox_f, hgrn_lower, nsa_cmp_w, lru_conv_w,
           lru_conv_b, lru_wa, lru_ba, lru_wx, lru_bx, lru_lambda, g_mix, w_out, g_mem, xa_wq, xa_wkv, xa_wo,
           w_ff1, w_ff2, norms):
    assert x_prompt.shape[0] == 1 and mem_prompt.shape[0] == 1
    w = dict(w_in=w_in, b_fox_f=b_fox_f, nsa_cmp_w=nsa_cmp_w, lru_conv_w=lru_conv_w, lru_conv_b=lru_conv_b,
             lru_wa=lru_wa, lru_ba=lru_ba, lru_wx=lru_wx, lru_bx=lru_bx, lru_lambda=lru_lambda, g_mix=g_mix,
             w_out=w_out, g_mem=g_mem, xa_wq=xa_wq, xa_wkv=xa_wkv, xa_wo=xa_wo, w_ff1=w_ff1, w_ff2=w_ff2,
             norms=norms)
    depth = w_in.shape[0]
    y_p, y_s = x_prompt[0], x_sample
    p_out, s_out = [], []
    for l in range(depth):
        wl = _layer_weights(w, l)
        y_p, po = _prompt_layer(y_p, mem_prompt[0], wl, l, hgrn_lower)
        y_s, so = _sample_layer(y_s, wl, l, hgrn_lower, page_table, cache_fox_kv, cache_fox_logf, cache_nsa_kv,
                                state_nsa_win[l], state_hgrn[l], state_lru_h[l], state_lru_conv[l], cache_xa_kv[l])
        p_out.append(po)
        s_out.append(so)
    stack = lambda outs, i: jnp.stack([o[i] for o in outs], axis=0)
    return ((y_p[None], y_s) + tuple(stack(p_out, i) for i in range(8)) + tuple(stack(s_out, i) for i in range(7)))
```

```python
import functools

import jax
import jax.numpy as jnp
import numpy as np
from jax import lax
from jax.experimental import pallas as pl
from jax.experimental.pallas import tpu as pltpu

F32 = jnp.float32
BF16 = jnp.bfloat16
I32 = jnp.int32

D_MODEL = 1024
GROUP_W = 256
HEAD_DIM = 64
N_HEADS = 4
NSA_CMP_BLOCK = 32
NSA_SEL_BLOCK = 64
NSA_TOPK = 16
NSA_WINDOW = 512
LRU_CONV = 4
LRU_C = 8.0
D_FF = 4096
EPS = 1e-6
NEG_BIG = -1e30
LB_FLOOR = 1e-30
SCALE = HEAD_DIM ** -0.5
LOG2E = float(np.log2(np.e))
PAGE = 128

V7X_VMEM_BYTES = 64 * 1024 * 1024
VMEM_LIMIT = V7X_VMEM_BYTES - 8 * 1024 * 1024
LANES = 128

C_FQ, C_FK, C_FV = 0, 256, 512
C_HQ, C_HF, C_HI, C_HG = 768, 1024, 1280, 1536
C_NQ = 1792
C_UX, C_UG = 2048, 2304
C_NKV = 2560
C_FF, C_NG = 2944, 2948
D_PROJ_PAD = 3072


def _params(*sem):
    return pltpu.CompilerParams(dimension_semantics=sem, vmem_limit_bytes=VMEM_LIMIT)


def _const_spec(shape):
    n = len(shape)
    return pl.BlockSpec(shape, lambda *_: (0,) * n)


def _rms(x, g):
    return x * lax.rsqrt(jnp.mean(x * x, axis=-1, keepdims=True) + EPS) * g


def _sigmoid(x):
    return 1.0 / (1.0 + jnp.exp(-x))


def _softplus(x):
    return jnp.maximum(x, 0.0) + jnp.log1p(jnp.exp(-jnp.abs(x)))


def _log_sigmoid(x):
    return -_softplus(-x)


def _gelu_tanh(x):
    return 0.5 * x * (1.0 + jnp.tanh(np.sqrt(2.0 / np.pi) * (x + 0.044715 * (x * x * x))))


def _trunc_bf16(x):
    bits = lax.bitcast_convert_type(x, jnp.uint32) & jnp.uint32(0xFFFF0000)
    return lax.bitcast_convert_type(bits, F32)


def _split3(x):
    hi = _trunc_bf16(x)
    r = x - hi
    mid = _trunc_bf16(r)
    lo = _trunc_bf16(r - mid)
    return hi.astype(BF16), mid.astype(BF16), lo.astype(BF16)


def _dot(a, b):
    return jnp.dot(a, b, preferred_element_type=F32)


def _dot_nt(a, b):
    return lax.dot_general(a, b, (((1,), (1,)), ((), ())), preferred_element_type=F32)


def _dot_tn(a, b):
    return lax.dot_general(a, b, (((0,), (0,)), ((), ())), preferred_element_type=F32)


def _dot_exact_lhs(x, m01):
    hi, mid, lo = _split3(x)
    return _dot(hi, m01) + _dot(mid, m01) + _dot(lo, m01)


def _dot_exact_rhs(m01, x):
    hi, mid, lo = _split3(x)
    return _dot(m01, hi) + _dot(m01, mid) + _dot(m01, lo)


def _rms_matmul_kernel(x_ref, g_ref, w_ref, o_ref):
    h = _rms(x_ref[...], g_ref[...])
    o_ref[...] = _dot(h.astype(BF16), w_ref[...])


def rms_matmul(x, g, w, tm):
    m, k = x.shape
    n = w.shape[1]
    return pl.pallas_call(
        _rms_matmul_kernel,
        grid=(m // tm,),
        in_specs=[pl.BlockSpec((tm, k), lambda i: (i, 0)), _const_spec((1, k)), _const_spec((k, n))],
        out_specs=pl.BlockSpec((tm, n), lambda i: (i, 0)),
        out_shape=jax.ShapeDtypeStruct((m, n), F32),
        compiler_params=_params("parallel"),
        name="rms_matmul",
    )(x, g.reshape(1, k), w)


def _logf_kernel(b_ref, x_ref, u_ref, ls_ref, logf_ref, c_ref, *, valid, chain):
    h = pl.program_id(0)
    lf = _log_sigmoid(x_ref[0] + b_ref[h])
    n = lf.shape[0]
    if valid < LANES:
        lane = lax.broadcasted_iota(I32, (n, LANES), 1)
        lf = jnp.where(lane < valid, lf, 0.0)
    logf_ref[0] = lf
    c = _dot_exact_lhs(lf, u_ref[...])
    if chain:
        tot = jnp.broadcast_to(c[:, LANES - 1:LANES], (n, LANES))
        c = c + _dot_exact_rhs(ls_ref[...], tot)
    c_ref[0] = c


def logf_cumsum(ff, bias, valid, chain):
    nh, n, _ = ff.shape
    u = jnp.triu(jnp.ones((LANES, LANES), F32)).astype(BF16)
    ls = jnp.tril(jnp.ones((n, n), F32), -1).astype(BF16)
    blk = pl.BlockSpec((1, n, LANES), lambda h: (h, 0, 0))
    return pl.pallas_call(
        functools.partial(_logf_kernel, valid=valid, chain=chain),
        grid=(nh,),
        in_specs=[pl.BlockSpec(memory_space=pltpu.SMEM), blk, _const_spec((LANES, LANES)), _const_spec((n, n))],
        out_specs=[blk, blk],
        out_shape=[jax.ShapeDtypeStruct(ff.shape, F32)] * 2,
        compiler_params=_params("parallel"),
        name="logf_cumsum",
    )(bias, ff, u, ls)


SEL_WINDOW = LANES


def _flash_kernel(*refs, tq, tk, hstack, window, with_sel):
    if with_sel:
        q_ref, sel_ref, k_ref, v_ref, o_ref, m_sc, acc_sc = refs
    else:
        q_ref, k_ref, v_ref, o_ref, m_sc, acc_sc = refs
    i = pl.program_id(1)
    cols = hstack * tq
    n_kv = k_ref.shape[1]
    q = q_ref[0]
    m_sc[...] = jnp.full((1, cols), NEG_BIG, F32)
    acc_sc[...] = jnp.zeros((LANES, cols), F32)
    qs = i * tq
    jd = qs // tk
    qpos = qs + lax.broadcasted_iota(I32, (1, cols), 1) % tq

    def softmax_pv(j, s, masked):
        if masked:
            kpos = j * tk + lax.broadcasted_iota(I32, (tk, 1), 0)
            valid = kpos <= qpos
            if window is not None:
                valid = jnp.logical_and(valid, qpos - kpos < window)
            s = jnp.where(valid, s, NEG_BIG)
        m_old = m_sc[...]
        m_new = jnp.maximum(m_old, jnp.max(s, axis=0, keepdims=True))
        p = jnp.exp2(s - m_new)
        if masked:
            p = jnp.where(valid, p, 0.0)
        alpha = jnp.exp2(m_old - m_new)
        acc_sc[...] = alpha * acc_sc[...] + _dot(v_ref[0, j], p.astype(BF16))
        m_sc[...] = m_new

    def step(j, qw, masked):
        softmax_pv(j, _dot(k_ref[0, j], qw), masked)

    def loop(lo, hi, qw, masked):
        def body(j, carry):
            step(j, qw, masked)
            return carry
        lax.fori_loop(lo, hi, body, 0)

    if with_sel:
        per_win = SEL_WINDOW * NSA_SEL_BLOCK // tk
        for w in range(-(-n_kv // per_win)):
            lo_w, hi_w = w * per_win, min((w + 1) * per_win, n_kv)
            sel = sel_ref[0, w * SEL_WINDOW:(w + 1) * SEL_WINDOW, :]
            qw = jnp.concatenate([q, jnp.concatenate([sel] * hstack, axis=1)], axis=0)
            loop(lo_w, jnp.clip(jd, lo_w, hi_w), qw, False)

            @pl.when(jnp.logical_and(jd >= lo_w, jd < hi_w))
            def _():
                step(jd, qw, True)
    elif window is None:
        loop(0, jd, q, False)
        step(jd, q, True)
    else:
        loop(jnp.maximum(jd - window // tk, 0), jd + 1, q, True)
    acc = acc_sc[...]
    den = acc[HEAD_DIM:HEAD_DIM + 1, :]
    o_ref[0] = acc[0:HEAD_DIM, :] / jnp.where(den > 0.0, den, 1.0)


def flash_attention(qt, k, vt, sel=None, *, tq, hstack, window=None):
    g, kq, qcols = qt.shape
    gk, n_kv, tk, kd = k.shape
    cols = hstack * tq
    kmap = (lambda a, i: (a, 0, 0, 0)) if gk > 1 else (lambda a, i: (0, 0, 0, 0))
    in_specs = [pl.BlockSpec((1, kq, cols), lambda a, i: (a, 0, i))]
    args = [qt]
    if sel is not None:
        in_specs.append(pl.BlockSpec((1, sel.shape[1], tq), lambda a, i: (a, 0, i)))
        args.append(sel)
    in_specs += [pl.BlockSpec((1, n_kv, tk, kd), kmap), pl.BlockSpec((1, n_kv, LANES, tk), kmap)]
    return pl.pallas_call(
        functools.partial(_flash_kernel, tq=tq, tk=tk, hstack=hstack, window=window, with_sel=sel is not None),
        grid=(g, qcols // cols),
        in_specs=in_specs,
        out_specs=pl.BlockSpec((1, HEAD_DIM, cols), lambda a, i: (a, 0, i)),
        out_shape=jax.ShapeDtypeStruct((g, HEAD_DIM, qcols), F32),
        scratch_shapes=[pltpu.VMEM((1, cols), F32), pltpu.VMEM((LANES, cols), F32)],
        compiler_params=_params("parallel", "arbitrary"),
        name="flash_attention",
    )(*args, k, vt)


def _compress_kernel(kv_ref, w_ref, o_ref):
    kv = kv_ref[...]
    n = kv.shape[0] // NSA_CMP_BLOCK
    blocks = kv.reshape(n, NSA_CMP_BLOCK, LANES) * w_ref[...][None]
    o_ref[...] = jnp.sum(blocks, axis=1)


def nsa_compress(z, w2, tl):
    length = z.shape[0]
    return pl.pallas_call(
        _compress_kernel,
        grid=(length // tl,),
        in_specs=[pl.BlockSpec((tl, LANES), lambda i: (i, C_NKV // LANES)), _const_spec((NSA_CMP_BLOCK, LANES))],
        out_specs=pl.BlockSpec((tl // NSA_CMP_BLOCK, LANES), lambda i: (i, 0)),
        out_shape=jax.ShapeDtypeStruct((length // NSA_CMP_BLOCK, LANES), F32),
        compiler_params=_params("parallel"),
        name="nsa_compress",
    )(z, w2)


def _cmp_topk_kernel(q_ref, kv_ref, oc_ref, sel_ref, idx_ref, *, tq, pos0, n_rounds):
    i = pl.program_id(1)
    qt = q_ref[0]
    kv = kv_ref[0]
    n = kv.shape[1]
    nsc = n // 2
    kc = jnp.transpose(kv[:HEAD_DIM]).astype(BF16)
    vc = kv[HEAD_DIM:].astype(BF16)
    pos = pos0 + i * tq + lax.broadcasted_iota(I32, (1, tq), 1)
    slot = lax.broadcasted_iota(I32, (n, 1), 0)
    cmp_idx = jnp.where(slot < nsc, 2 * slot, 2 * (slot - nsc) + 1)
    valid = (cmp_idx + 1) * NSA_CMP_BLOCK - 1 <= pos
    imp = jnp.zeros((n, tq), F32)
    for h in range(N_HEADS):
        rows = slice(h * HEAD_DIM, (h + 1) * HEAD_DIM)
        s = jnp.where(valid, _dot(kc, qt[rows].astype(BF16)) * SCALE, NEG_BIG)
        m = jnp.max(s, axis=0, keepdims=True)
        e = jnp.where(valid, jnp.exp(s - m), 0.0)
        d = jnp.sum(e, axis=0, keepdims=True)
        p = e / jnp.where(d > 0.0, d, 1.0)
        oc_ref[0, rows, :] = _dot(vc, p.astype(BF16))
        imp = imp + p

    imp = imp[:nsc] + imp[nsc:]
    blk = lax.broadcasted_iota(I32, (nsc, 1), 0)
    blk_f = blk.astype(F32)
    forced = jnp.logical_or(blk == pos // NSA_SEL_BLOCK, blk == 0)
    causal = blk * NSA_SEL_BLOCK <= pos
    imp = jnp.where(forced, 1e9, jnp.where(causal, imp, -1e9))
    sel = jnp.zeros((nsc, tq), F32)
    rnd = lax.broadcasted_iota(I32, (idx_ref.shape[1], tq), 0)
    idx = jnp.full(rnd.shape, -1, I32)
    for r in range(n_rounds):
        mx = jnp.max(imp, axis=0, keepdims=True)
        first = jnp.min(jnp.where(imp == mx, blk_f, float(nsc)), axis=0, keepdims=True)
        hit = blk_f == first
        ok = mx >= 0.0
        sel = jnp.where(jnp.logical_and(hit, ok), 1.0, sel)
        idx = jnp.where(rnd == r, jnp.where(ok, first.astype(I32), -1), idx)
        imp = jnp.where(hit, -3e38, imp)
    sel_ref[0] = jnp.where(sel > 0.0, 0.0, NEG_BIG).astype(BF16)
    idx_ref[0] = idx


def cmp_topk(qt, kvt, *, tq, pos0, n_rounds):
    g, _, lq = qt.shape
    gk, _, n = kvt.shape
    nsc = n // 2
    kmap = (lambda a, i: (a, 0, 0)) if gk > 1 else (lambda a, i: (0, 0, 0))
    col = lambda r: pl.BlockSpec((1, r, tq), lambda a, i: (a, 0, i))
    return pl.pallas_call(
        functools.partial(_cmp_topk_kernel, tq=tq, pos0=pos0, n_rounds=n_rounds),
        grid=(g, lq // tq),
        in_specs=[col(GROUP_W), pl.BlockSpec((1, LANES, n), kmap)],
        out_specs=[col(GROUP_W), col(nsc), col(NSA_TOPK)],
        out_shape=[jax.ShapeDtypeStruct((g, GROUP_W, lq), F32), jax.ShapeDtypeStruct((g, nsc, lq), BF16),
                   jax.ShapeDtypeStruct((g, NSA_TOPK, lq), I32)],
        compiler_params=_params("parallel", "parallel"),
        name="cmp_topk",
    )(qt, kvt)


HG_SUB = 16
HG_PAD = 16
HG_PAIR = 2 * HEAD_DIM


def _hgrn_kernel(q_ref, f_ref, i_ref, g_ref, low_ref, gain_ref, s0_ref, tril_ref, ones_ref,
                 o_ref, s_out_ref, st_sc, ksh_sc, bsh_sc, vsh_sc, *, layer, chunk, n_chunks, valid_len):
    step = pl.program_id(1)
    sub = min(HG_SUB, chunk)

    @pl.when(step == 0)
    def _():
        st_sc[...] = s0_ref[0]
        zeros = jnp.zeros((HG_PAD, HG_PAIR), F32)
        ksh_sc[0:HG_PAD, :] = zeros
        bsh_sc[0:HG_PAD, :] = zeros
        vsh_sc[0:HG_PAD, :] = zeros

    low = low_ref[...]
    e = jnp.exp(low - jnp.max(low, axis=0, keepdims=True))
    lower = e / jnp.sum(e, axis=0, keepdims=True)
    cs = lower[0:1]
    for d in range(1, layer + 1):
        cs = cs + lower[d:d + 1]
    lb_all = cs - lower[0:1]
    tril = tril_ref[...]
    ones_bd = ones_ref[...]
    row = lax.broadcasted_iota(I32, (chunk, 1), 0)

    def do_chunk(c, carry):
        r0 = pl.multiple_of(c * chunk, chunk)
        base = step * (n_chunks * chunk) + r0
        live = (base + row) < valid_len
        outs = []
        for pair in range(N_HEADS // 2):
            pc = slice(pair * HG_PAIR, (pair + 1) * HG_PAIR)
            lb = lb_all[:, pc]
            zq = q_ref[0, pl.ds(r0, chunk), pc]
            zf = f_ref[0, pl.ds(r0, chunk), pc]
            v = i_ref[0, pl.ds(r0, chunk), pc]
            q = zq * _sigmoid(zq)
            a = jnp.log1p(-lb) + _log_sigmoid(zf)
            b0 = jnp.log(jnp.maximum(lb, LB_FLOOR))
            log_f = jnp.maximum(a, b0) + jnp.log1p(jnp.exp(-jnp.abs(a - b0)))
            k = (1.0 - lb) * _sigmoid(-zf)
            log_f = jnp.where(live, log_f, 0.0)
            k = jnp.where(live, k, 0.0)
            b = _dot_exact_rhs(tril, log_f)
            ksh_sc[HG_PAD:HG_PAD + chunk, :] = k
            bsh_sc[HG_PAD:HG_PAD + chunk, :] = b
            vsh_sc[HG_PAD:HG_PAD + chunk, :] = v
            o2 = jnp.zeros((chunk, HG_PAIR), F32)
            for d in range(sub):
                kd = ksh_sc[HG_PAD - d:HG_PAD - d + chunk, :]
                bd = bsh_sc[HG_PAD - d:HG_PAD - d + chunk, :]
                vd = vsh_sc[HG_PAD - d:HG_PAD - d + chunk, :]
                ok = (row % sub) >= d
                w = jnp.where(ok, q * kd * jnp.exp(jnp.where(ok, b - bd, 0.0)), 0.0)
                o2 = o2 + _dot(w.astype(BF16), ones_bd) * vd
            qe = q * jnp.exp(b)
            b_last = b[chunk - 1:chunk]
            kl = k * jnp.exp(b_last - b)
            e_last = jnp.exp(b_last)
            for hh in range(2):
                h = 2 * pair + hh
                hc = slice(hh * HEAD_DIM, (hh + 1) * HEAD_DIM)
                st = st_sc[h]
                qh, kh, bh, vh = q[:, hc], k[:, hc], b[:, hc], v[:, hc]
                o = o2[:, hc] + _dot_nt(qe[:, hc].astype(BF16), st.astype(BF16))
                parts = [o[0:sub]]
                for s_i in range(1, chunk // sub):
                    lo = s_i * sub
                    b_start = bh[lo - 1:lo]
                    qd = qh[lo:lo + sub] * jnp.exp(bh[lo:lo + sub] - b_start)
                    kdec = kh[0:lo] * jnp.exp(b_start - bh[0:lo])
                    att = _dot_nt(qd.astype(BF16), kdec.astype(BF16))
                    parts.append(o[lo:lo + sub] + _dot(att.astype(BF16), vh[0:lo].astype(BF16)))
                o = jnp.concatenate(parts, axis=0) if len(parts) > 1 else parts[0]
                st_sc[h] = e_last[:, hc] * st + _dot_tn(vh.astype(BF16), kl[:, hc].astype(BF16))
                outs.append(_rms(o, gain_ref[:, h * HEAD_DIM:(h + 1) * HEAD_DIM]))
        zg = g_ref[0, pl.ds(r0, chunk), :]
        o_ref[0, pl.ds(r0, chunk), :] = jnp.concatenate(outs, axis=-1) * (zg * _sigmoid(zg))
        return carry

    lax.fori_loop(0, n_chunks, do_chunk, 0)

    @pl.when(step == pl.num_programs(1) - 1)
    def _():
        s_out_ref[0] = st_sc[...]


def hgrn(z, hgrn_lower, gain, s0t, *, layer, chunk, n_chunks, valid_len):
    bsz, lp, _ = z.shape
    tb = chunk * n_chunks
    tril = jnp.tril(jnp.ones((chunk, chunk), F32)).astype(BF16)
    lane_head = np.arange(HG_PAIR) // HEAD_DIM
    ones_bd = jnp.asarray(lane_head[:, None] == lane_head[None, :], BF16)
    col = lambda c: pl.BlockSpec((1, tb, GROUP_W), lambda b, i: (b, i, c // GROUP_W))
    st_spec = pl.BlockSpec((1, N_HEADS, HEAD_DIM, HEAD_DIM), lambda b, i: (b, 0, 0, 0))
    depth = hgrn_lower.shape[0]
    return pl.pallas_call(
        functools.partial(_hgrn_kernel, layer=layer, chunk=chunk, n_chunks=n_chunks, valid_len=valid_len),
        grid=(bsz, lp // tb),
        in_specs=[col(C_HQ), col(C_HF), col(C_HI), col(C_HG), _const_spec((depth, GROUP_W)),
                  _const_spec((1, GROUP_W)), st_spec, _const_spec((chunk, chunk)), _const_spec((HG_PAIR, HG_PAIR))],
        out_specs=[pl.BlockSpec((1, tb, GROUP_W), lambda b, i: (b, i, 0)), st_spec],
        out_shape=[jax.ShapeDtypeStruct((bsz, lp, GROUP_W), F32),
                   jax.ShapeDtypeStruct((bsz, N_HEADS, HEAD_DIM, HEAD_DIM), F32)],
        scratch_shapes=[pltpu.VMEM((N_HEADS, HEAD_DIM, HEAD_DIM), F32)]
                       + [pltpu.VMEM((HG_PAD + chunk, HG_PAIR), F32)] * 3,
        compiler_params=_params("parallel", "arbitrary"),
        name="hgrn2",
    )(z, z, z, z, hgrn_lower, gain.reshape(1, GROUP_W), s0t, tril, ones_bd)


LRU_PAD = 8


def _rglru_kernel(ux_ref, ug_ref, conv0_ref, h0_ref, cw_ref, cb_ref, wa_ref, ba_ref, wx_ref, bx_ref,
                  lam_ref, y_ref, hl_ref, xp_sc, a_sc, b_sc, h_sc, hc_sc, *, tb):
    step = pl.program_id(1)

    @pl.when(step == 0)
    def _():
        xp_sc[0:LRU_PAD, :] = conv0_ref[0]
        hc_sc[...] = h0_ref[0]

    x = ux_ref[0]
    xp_sc[LRU_PAD:LRU_PAD + tb, :] = x
    cw = cw_ref[...]
    u = cb_ref[...] + xp_sc[LRU_PAD - 3:LRU_PAD - 3 + tb, :] * cw[0:1]
    for j in range(1, LRU_CONV):
        u = u + xp_sc[LRU_PAD - 3 + j:LRU_PAD - 3 + j + tb, :] * cw[j:j + 1]
    if tb >= LRU_PAD:
        xp_sc[0:LRU_PAD, :] = xp_sc[tb:tb + LRU_PAD, :]
    ub = u.astype(BF16)
    r = _sigmoid(_dot(ub, wa_ref[...]) + ba_ref[...])
    gi = _sigmoid(_dot(ub, wx_ref[...]) + bx_ref[...])
    log_a = -LRU_C * r * _softplus(-lam_ref[...])
    a = jnp.exp(log_a)
    a_sc[...] = a
    b_sc[...] = jnp.sqrt(1.0 - a * a) * (gi * u)

    def body(t, h):
        h = a_sc[pl.ds(t, 1), :] * h + b_sc[pl.ds(t, 1), :]
        h_sc[pl.ds(t, 1), :] = h
        return h

    h = lax.fori_loop(0, tb, body, hc_sc[...], unroll=min(tb, 8))
    hc_sc[...] = h
    y_ref[0] = h_sc[...] * _gelu_tanh(ug_ref[0])
    hl_ref[0] = h


def rglru(z, conv0, h0, cw, cb, wa_bd, ba, wx_bd, bx, lam, *, tb):
    bsz, length, _ = z.shape
    col = lambda c: pl.BlockSpec((1, tb, GROUP_W), lambda b, i: (b, i, c // GROUP_W))
    per_b = lambda r: pl.BlockSpec((1, r, GROUP_W), lambda b, i: (b, 0, 0))
    vec = _const_spec((1, GROUP_W))
    mat = _const_spec((GROUP_W, GROUP_W))
    return pl.pallas_call(
        functools.partial(_rglru_kernel, tb=tb),
        grid=(bsz, length // tb),
        in_specs=[col(C_UX), col(C_UG), per_b(LRU_PAD), per_b(1), _const_spec((LRU_CONV, GROUP_W)), vec,
                  mat, vec, mat, vec, vec],
        out_specs=[pl.BlockSpec((1, tb, GROUP_W), lambda b, i: (b, i, 0)), per_b(1)],
        out_shape=[jax.ShapeDtypeStruct((bsz, length, GROUP_W), F32), jax.ShapeDtypeStruct((bsz, 1, GROUP_W), F32)],
        scratch_shapes=[pltpu.VMEM((LRU_PAD + tb + LRU_PAD, GROUP_W), F32), pltpu.VMEM((tb, GROUP_W), F32),
                        pltpu.VMEM((tb, GROUP_W), F32), pltpu.VMEM((tb, GROUP_W), F32),
                        pltpu.VMEM((1, GROUP_W), F32)],
        compiler_params=_params("parallel", "arbitrary"),
        name="rglru",
    )(z, z, conv0, h0, cw, cb.reshape(1, -1), wa_bd, ba.reshape(1, -1), wx_bd, bx.reshape(1, -1),
      lam.reshape(1, -1))


def _mix_kernel(x_ref, oa_ref, ob_ref, oc_ref, os_ref, ow_ref, od_ref, zl_ref, gm_ref, w_ref, n_ref, y_ref):
    gm = gm_ref[...]
    gates = _sigmoid(zl_ref[...][:, C_NG - C_FF:C_NG - C_FF + 3 * N_HEADS])
    oc, osel, ow = oc_ref[...], os_ref[...], ow_ref[...]
    parts = []
    for h in range(N_HEADS):
        cols = slice(h * HEAD_DIM, (h + 1) * HEAD_DIM)
        parts.append(gates[:, 3 * h:3 * h + 1] * oc[:, cols] + gates[:, 3 * h + 1:3 * h + 2] * osel[:, cols]
                     + gates[:, 3 * h + 2:3 * h + 3] * ow[:, cols])
    o_nsa = jnp.concatenate(parts, axis=-1)
    groups = (_rms(oa_ref[...], gm[:, 0:GROUP_W]), ob_ref[...],
              _rms(o_nsa, gm[:, 2 * GROUP_W:3 * GROUP_W]), _rms(od_ref[...], gm[:, 3 * GROUP_W:]))
    y = jnp.zeros(x_ref.shape, F32)
    for gi, grp in enumerate(groups):
        y = y + _dot(grp.astype(BF16), w_ref[gi * GROUP_W:(gi + 1) * GROUP_W, :])
    y_ref[...] = x_ref[...] + _rms(y, n_ref[...])


def mix(x, oa, ob, oc, osel, ow, od, z, gm, w_out, n1, tm):
    m = x.shape[0]
    row = lambda w: pl.BlockSpec((tm, w), lambda i: (i, 0))
    return pl.pallas_call(
        _mix_kernel,
        grid=(m // tm,),
        in_specs=[row(D_MODEL)] + [row(GROUP_W)] * 6
                 + [pl.BlockSpec((tm, LANES), lambda i: (i, C_FF // LANES)), _const_spec((1, D_MODEL)),
                    _const_spec((D_MODEL, D_MODEL)), _const_spec((1, D_MODEL))],
        out_specs=row(D_MODEL),
        out_shape=jax.ShapeDtypeStruct((m, D_MODEL), F32),
        compiler_params=_params("parallel"),
        name="mix_residual",
    )(x, oa, ob, oc, osel, ow, od, z, gm.reshape(1, -1), w_out, n1.reshape(1, -1))


def _xattn_kernel(x_ref, n2_ref, wq_ref, kv_ref, wo_ref, n3_ref, y_ref):
    x = x_ref[0]
    q = _dot(_rms(x, n2_ref[...]).astype(BF16), wq_ref[...])
    kv = kv_ref[0].astype(BF16)
    hw = N_HEADS * HEAD_DIM
    o = jnp.zeros(x.shape, F32)
    for h in range(N_HEADS):
        cols = slice(h * HEAD_DIM, (h + 1) * HEAD_DIM)
        s = _dot_nt(q[:, cols].astype(BF16), kv[:, cols]) * SCALE
        e = jnp.exp(s - jnp.max(s, axis=-1, keepdims=True))
        p = e / jnp.sum(e, axis=-1, keepdims=True)
        oh = _dot(p.astype(BF16), kv[:, hw + h * HEAD_DIM:hw + (h + 1) * HEAD_DIM])
        o = o + _dot(oh.astype(BF16), wo_ref[cols, :])
    y_ref[0] = x + _rms(o, n3_ref[...])


def xattn(x, n2, wq, kv, wo, n3, tm):
    g, m, _ = x.shape
    gk, mem, kvw = kv.shape
    kmap = (lambda a, i: (a, 0, 0)) if gk > 1 else (lambda a, i: (0, 0, 0))
    xs = pl.BlockSpec((1, tm, D_MODEL), lambda a, i: (a, i, 0))
    return pl.pallas_call(
        _xattn_kernel,
        grid=(g, m // tm),
        in_specs=[xs, _const_spec((1, D_MODEL)), _const_spec((D_MODEL, GROUP_W)), pl.BlockSpec((1, mem, kvw), kmap),
                  _const_spec((GROUP_W, D_MODEL)), _const_spec((1, D_MODEL))],
        out_specs=xs,
        out_shape=jax.ShapeDtypeStruct(x.shape, F32),
        compiler_params=_params("parallel", "parallel"),
        name="cross_attention",
    )(x, n2.reshape(1, -1), wq, kv, wo, n3.reshape(1, -1))


FF_CHUNK = 1024


def _ffn_kernel(x_ref, n4_ref, w1_ref, w2_ref, n5_ref, y_ref):
    x = x_ref[...]
    h = _rms(x, n4_ref[...]).astype(BF16)
    f = jnp.zeros(x.shape, F32)
    for c in range(D_FF // FF_CHUNK):
        cols = slice(c * FF_CHUNK, (c + 1) * FF_CHUNK)
        a = jnp.maximum(_dot(h, w1_ref[:, cols]), 0.0)
        f = f + _dot((a * a).astype(BF16), w2_ref[cols, :])
    y_ref[...] = x + _rms(f, n5_ref[...])


def ffn(x, n4, w1, w2, n5, tm):
    m = x.shape[0]
    row = pl.BlockSpec((tm, D_MODEL), lambda i: (i, 0))
    once = pl.Buffered(1)
    return pl.pallas_call(
        _ffn_kernel,
        grid=(m // tm,),
        in_specs=[row, _const_spec((1, D_MODEL)),
                  pl.BlockSpec((D_MODEL, D_FF), lambda i: (0, 0), pipeline_mode=once),
                  pl.BlockSpec((D_FF, D_MODEL), lambda i: (0, 0), pipeline_mode=once),
                  _const_spec((1, D_MODEL))],
        out_specs=row,
        out_shape=jax.ShapeDtypeStruct((m, D_MODEL), F32),
        compiler_params=_params("parallel"),
        name="ffn",
    )(x, n4.reshape(1, -1), w1, w2, n5.reshape(1, -1))


FOX_PG = 8


def _fox_sample_kernel(pt_ref, q_ref, cnb_ref, cnk_ref, kn_ref, vn_ref, u_ref, *refs, n_tok):
    kv_refs = refs[:FOX_PG]
    lf_refs = refs[FOX_PG:2 * FOX_PG]
    o_ref = refs[2 * FOX_PG]
    m_sc, acc_sc, carry_sc = refs[2 * FOX_PG + 1:]
    g = pl.program_id(1)
    rows = N_HEADS * n_tok
    q = q_ref[0]
    cnb = cnb_ref[0]

    def update(s, valid, pv):
        m_old = m_sc[...]
        m_new = jnp.maximum(m_old, jnp.max(s, axis=-1, keepdims=True))
        p = jnp.exp(s - m_new)
        if valid is not None:
            p = jnp.where(valid, p, 0.0)
        alpha = jnp.exp(m_old - m_new)
        l_add = jnp.sum(p, axis=-1, keepdims=True)
        acc = acc_sc[...]
        acc_sc[:, 0:GROUP_W] = alpha * acc[:, 0:GROUP_W] + pv(p.astype(BF16))
        acc_sc[:, GROUP_W:] = alpha * acc[:, GROUP_W:] + l_add
        m_sc[...] = m_new

    @pl.when(g == 0)
    def _():
        m_sc[...] = jnp.full((rows, 1), NEG_BIG, F32)
        acc_sc[...] = jnp.zeros(acc_sc.shape, F32)
        carry_sc[...] = jnp.zeros(carry_sc.shape, F32)
        nk = kn_ref.shape[1]
        t_idx = lax.broadcasted_iota(I32, (rows, nk), 0) % n_tok
        j_idx = lax.broadcasted_iota(I32, (rows, nk), 1)
        valid = jnp.logical_and(j_idx <= t_idx, j_idx < n_tok)
        s = _dot_nt(q, kn_ref[0]) + cnb[:, 0:nk] - cnk_ref[0][:, 0:nk]
        vn = vn_ref[0]
        update(jnp.where(valid, s, NEG_BIG), valid, lambda p: _dot(p, vn))

    lf = jnp.concatenate([r[0, 0] for r in lf_refs], axis=0)
    suffix = _dot_exact_lhs(lf, u_ref[...])
    total = jnp.broadcast_to(jnp.sum(lf, axis=-1, keepdims=True), lf.shape)
    carry = carry_sc[...]
    scores, vts = [], []
    for i in range(FOX_PG):
        rows_i = slice(i * N_HEADS, (i + 1) * N_HEADS)
        kt = kv_refs[i][0, 0, 0].reshape(GROUP_W, PAGE).astype(BF16)
        vts.append(kv_refs[i][0, 0, 1].reshape(GROUP_W, PAGE).astype(BF16))
        b4 = suffix[rows_i] + carry
        bias = jnp.concatenate([jnp.broadcast_to(b4[h:h + 1], (n_tok, PAGE)) for h in range(N_HEADS)], axis=0)
        scores.append(_dot(q, kt) + bias + cnb)
        carry = carry + total[rows_i]
    carry_sc[...] = carry

    def pv_all(p):
        out = _dot_nt(p[:, 0:PAGE], vts[0])
        for i in range(1, FOX_PG):
            out = out + _dot_nt(p[:, i * PAGE:(i + 1) * PAGE], vts[i])
        return out

    update(jnp.concatenate(scores, axis=1), None, pv_all)

    @pl.when(g == pl.num_programs(1) - 1)
    def _():
        acc = acc_sc[...]
        o = acc[:, 0:GROUP_W] / acc[:, GROUP_W:GROUP_W + 1]
        r_h = lax.broadcasted_iota(I32, (rows, GROUP_W), 0) // n_tok
        c_h = lax.broadcasted_iota(I32, (rows, GROUP_W), 1) // HEAD_DIM
        o = jnp.where(r_h == c_h, o, 0.0)
        out = o[0:n_tok]
        for h in range(1, N_HEADS):
            out = out + o[h * n_tok:(h + 1) * n_tok]
        o_ref[0] = out


def fox_sample(page_table, qbd, cnb, cnk, knew, vnew, cache_kvt, cache_logft, *, layer, n_tok):
    bsz, n_pages = page_table.shape
    rows = N_HEADS * n_tok
    ngrp = n_pages // FOX_PG
    u = jnp.tril(jnp.ones((PAGE, PAGE), F32), -1).astype(BF16)

    def page_map(i, ndim):
        return lambda b, g, pt: (layer, pt[b, n_pages - 1 - (g * FOX_PG + i)]) + (0,) * (ndim - 2)

    per_b = lambda shape: pl.BlockSpec((1,) + shape, lambda b, g, pt: (b, 0, 0))
    grid_spec = pltpu.PrefetchScalarGridSpec(
        num_scalar_prefetch=1,
        grid=(bsz, ngrp),
        in_specs=[per_b((rows, GROUP_W)), per_b((rows, LANES)), per_b((rows, LANES)), per_b((8, GROUP_W)),
                  per_b((8, GROUP_W)), pl.BlockSpec((PAGE, PAGE), lambda b, g, pt: (0, 0))]
                 + [pl.BlockSpec((1, 1, 2, N_HEADS, HEAD_DIM, PAGE), page_map(i, 6)) for i in range(FOX_PG)]
                 + [pl.BlockSpec((1, 1, N_HEADS, PAGE), page_map(i, 4)) for i in range(FOX_PG)],
        out_specs=pl.BlockSpec((1, n_tok, GROUP_W), lambda b, g, pt: (b, 0, 0)),
        scratch_shapes=[pltpu.VMEM((rows, 1), F32), pltpu.VMEM((rows, GROUP_W + LANES), F32),
                        pltpu.VMEM((N_HEADS, PAGE), F32)],
    )
    return pl.pallas_call(
        functools.partial(_fox_sample_kernel, n_tok=n_tok),
        grid_spec=grid_spec,
        out_shape=jax.ShapeDtypeStruct((bsz, n_tok, GROUP_W), F32),
        compiler_params=_params("parallel", "arbitrary"),
        name="fox_sample",
    )(page_table, qbd, cnb, cnk, knew, vnew, u, *([cache_kvt] * FOX_PG), *([cache_logft] * FOX_PG))


CMP_PG = 8
CMP_PER_PAGE = PAGE // NSA_CMP_BLOCK


def _cmp_pages_kernel(pt_ref, w_ref, g_ref, *refs):
    o_ref = refs[CMP_PG]
    w = w_ref[...]
    x = jnp.concatenate([refs[i][0, 0].reshape(2 * HEAD_DIM, PAGE) * w for i in range(CMP_PG)], axis=1)
    o_ref[0, 0] = _dot_exact_lhs(x, g_ref[...])


def nsa_compress_pages(page_table, wt, cache_nsat, *, layer):
    bsz, n_pages = page_table.shape
    per_step = CMP_PG * CMP_PER_PAGE
    lane = np.arange(CMP_PG * PAGE)
    seg = jnp.asarray((lane // NSA_CMP_BLOCK)[:, None] == np.arange(per_step)[None, :], BF16)

    def page_map(i):
        return lambda b, g, pt: (layer, pt[b, g * CMP_PG + i], 0, 0, 0)

    grid_spec = pltpu.PrefetchScalarGridSpec(
        num_scalar_prefetch=1,
        grid=(bsz, n_pages // CMP_PG),
        in_specs=[pl.BlockSpec((2 * HEAD_DIM, PAGE), lambda b, g, pt: (0, 0)),
                  pl.BlockSpec(seg.shape, lambda b, g, pt: (0, 0))]
                 + [pl.BlockSpec((1, 1, 2, HEAD_DIM, PAGE), page_map(i)) for i in range(CMP_PG)],
        out_specs=pl.BlockSpec((1, 1, 2 * HEAD_DIM, per_step), lambda b, g, pt: (b, g, 0, 0)),
    )
    return pl.pallas_call(
        _cmp_pages_kernel,
        grid_spec=grid_spec,
        out_shape=jax.ShapeDtypeStruct((bsz, n_pages // CMP_PG, 2 * HEAD_DIM, per_step), F32),
        compiler_params=_params("parallel", "parallel"),
        name="nsa_compress_pages",
    )(page_table, wt, seg, *([cache_nsat] * CMP_PG))


def _slc_sample_kernel(idx_ref, pt_ref, q_ref, kn_ref, vn_ref, *refs, n_sel, n_tok):
    blk_refs = refs[:n_sel]
    o_ref = refs[n_sel]
    i = pl.program_id(0)
    t = i % n_tok
    q = q_ref[0]
    nq = q.shape[0]
    lane_half = lax.broadcasted_iota(I32, (nq, PAGE), 1) // NSA_SEL_BLOCK
    scores, masks = [], []
    for j in range(n_sel):
        half = idx_ref[i, j] % (PAGE // NSA_SEL_BLOCK)
        ok = lane_half == half
        s = _dot(q, blk_refs[j][0, 0, 0].astype(BF16)) * SCALE
        scores.append(jnp.where(ok, s, NEG_BIG))
        masks.append(ok)
    nk = kn_ref.shape[1]
    ok_new = lax.broadcasted_iota(I32, (nq, nk), 1) <= t
    s_new = jnp.where(ok_new, _dot_nt(q, kn_ref[0]) * SCALE, NEG_BIG)
    m = jnp.max(s_new, axis=-1, keepdims=True)
    for s in scores:
        m = jnp.maximum(m, jnp.max(s, axis=-1, keepdims=True))
    e_new = jnp.where(ok_new, jnp.exp(s_new - m), 0.0)
    es = [jnp.where(ok, jnp.exp(s - m), 0.0) for s, ok in zip(scores, masks)]
    d = jnp.sum(e_new, axis=-1, keepdims=True)
    for e in es:
        d = d + jnp.sum(e, axis=-1, keepdims=True)
    o = _dot((e_new / d).astype(BF16), vn_ref[0])
    for j, e in enumerate(es):
        o = o + _dot_nt((e / d).astype(BF16), blk_refs[j][0, 0, 1].astype(BF16))
    o_ref[0] = o


def slc_sample(idx, page_table, q, knew, vnew, cache_nsat, *, layer, n_sel, n_tok):
    n_q = q.shape[0]
    blocks_per_page = PAGE // NSA_SEL_BLOCK

    def blk_map(j):
        def f(i, idx_r, pt):
            return (layer, pt[i // n_tok, idx_r[i, j] // blocks_per_page], 1, 0, 0)
        return f

    per_b = lambda shape: pl.BlockSpec((1,) + shape, lambda i, idx_r, pt: (i // n_tok, 0, 0))
    grid_spec = pltpu.PrefetchScalarGridSpec(
        num_scalar_prefetch=2,
        grid=(n_q,),
        in_specs=[pl.BlockSpec((1, 8, HEAD_DIM), lambda i, idx_r, pt: (i, 0, 0)), per_b((8, HEAD_DIM)),
                  per_b((8, HEAD_DIM))]
                 + [pl.BlockSpec((1, 1, 2, HEAD_DIM, PAGE), blk_map(j)) for j in range(n_sel)],
        out_specs=pl.BlockSpec((1, 8, HEAD_DIM), lambda i, idx_r, pt: (i, 0, 0)),
    )
    return pl.pallas_call(
        functools.partial(_slc_sample_kernel, n_sel=n_sel, n_tok=n_tok),
        grid_spec=grid_spec,
        out_shape=jax.ShapeDtypeStruct((n_q, 8, HEAD_DIM), F32),
        compiler_params=_params("parallel"),
        name="nsa_slc_sample",
    )(idx, page_table, q, knew, vnew, *([cache_nsat] * n_sel))


def _win_sample_kernel(q_ref, k_ref, v_ref, o_ref, *, n_tok, n_win, n_keys):
    q = q_ref[0]
    rows, width = q.shape[0], k_ref.shape[1]
    t = lax.broadcasted_iota(I32, (rows, width), 0) % n_tok
    j = lax.broadcasted_iota(I32, (rows, width), 1)
    dist = n_win + t - j
    valid = jnp.logical_and(jnp.logical_and(dist >= 0, dist < NSA_WINDOW), j < n_keys)
    s = jnp.where(valid, _dot_nt(q, k_ref[0]) * SCALE, NEG_BIG)
    m = jnp.max(s, axis=-1, keepdims=True)
    e = jnp.where(valid, jnp.exp(s - m), 0.0)
    d = jnp.sum(e, axis=-1, keepdims=True)
    o_ref[0] = _dot((e / jnp.where(d > 0.0, d, 1.0)).astype(BF16), v_ref[0])


def win_sample(q, kw, vw, *, n_tok, n_win, n_keys):
    bsz, rows, _ = q.shape
    width = kw.shape[1]
    spec = lambda r: pl.BlockSpec((1, r, HEAD_DIM), lambda b: (b, 0, 0))
    return pl.pallas_call(
        functools.partial(_win_sample_kernel, n_tok=n_tok, n_win=n_win, n_keys=n_keys),
        grid=(bsz,),
        in_specs=[spec(rows), spec(width), spec(width)],
        out_specs=spec(rows),
        out_shape=jax.ShapeDtypeStruct((bsz, rows, HEAD_DIM), F32),
        compiler_params=_params("parallel"),
        name="nsa_win_sample",
    )(q, kw, vw)


def _split3_host(x):
    bits = lambda a: lax.bitcast_convert_type(
        lax.bitcast_convert_type(a, jnp.uint32) & jnp.uint32(0xFFFF0000), F32)
    hi = bits(x)
    r = x - hi
    mid = bits(r)
    lo = bits(r - mid)
    return hi.astype(BF16), mid.astype(BF16), lo.astype(BF16)


def _heads_t(a):
    return a.reshape(a.shape[0], N_HEADS, HEAD_DIM).transpose(1, 2, 0)


def _pad_rows(parts, height, dtype):
    used = sum(p.shape[-2] for p in parts)
    pad = jnp.zeros(parts[0].shape[:-2] + (height - used, parts[0].shape[-1]), dtype)
    return jnp.concatenate([p.astype(dtype) for p in parts] + [pad], axis=-2)


def _kv_tiles(k_rows, vt, tk):
    g, length, kd = k_rows.shape
    n = length // tk
    return (k_rows.reshape(g, n, tk, kd),
            vt.reshape(g, LANES, n, tk).transpose(0, 2, 1, 3))


def _q_cols_tiles(a, tq):
    length, _, w = a.shape
    return a.reshape(length // tq, tq, N_HEADS, w).transpose(3, 0, 2, 1).reshape(w, -1)


def _from_cols_tiles(o, tq):
    return o.reshape(HEAD_DIM, -1, N_HEADS, tq).transpose(1, 3, 2, 0).reshape(-1, GROUP_W)


def _evens_odds_cols(a):
    return jnp.concatenate([a[..., 0::2], a[..., 1::2]], axis=-1)


def _pick(n, candidates):
    for c in candidates:
        if n % c == 0:
            return c
    raise ValueError(f"no tile size for {n}")


def _layer_weights(w, l):
    w_in = w["w_in"][l]
    w_in_p = jnp.concatenate(
        [w_in[:, 0:768], w_in[:, 772:2052], w_in[:, 2448:2960], w_in[:, 2052:2436], w_in[:, 768:772],
         w_in[:, 2436:2448], jnp.zeros((D_MODEL, D_PROJ_PAD - 2960), F32)], axis=1).astype(BF16)
    bd = lambda m: jax.scipy.linalg.block_diag(*[m[i] for i in range(m.shape[0])]).astype(BF16)
    cmp_w = w["nsa_cmp_w"][l]
    return dict(
        w_in=w_in_p, n=w["norms"][l], b_fox=w["b_fox_f"][l],
        w2cmp=jnp.concatenate([cmp_w[0], cmp_w[1]], axis=-1),
        wtcmp=jnp.tile(cmp_w.transpose(0, 2, 1), (1, 1, CMP_PER_PAGE)).reshape(2 * HEAD_DIM, PAGE),
        cw=w["lru_conv_w"][l], cb=w["lru_conv_b"][l], wa=bd(w["lru_wa"][l]), ba=w["lru_ba"][l],
        wx=bd(w["lru_wx"][l]), bx=w["lru_bx"][l], lam=w["lru_lambda"][l], gm=w["g_mix"][l],
        w_out=w["w_out"][l].astype(BF16), g_mem=w["g_mem"][l], wq=w["xa_wq"][l].astype(BF16),
        wkv=w["xa_wkv"][l].astype(BF16), wo=w["xa_wo"][l].astype(BF16), w1=w["w_ff1"][l].astype(BF16),
        w2=w["w_ff2"][l].astype(BF16))


def _prompt_layer(x, mem, wl, l, hgrn_lower):
    length = x.shape[0]
    tm = _pick(length, (512, 256, 128))
    z = rms_matmul(x, wl["n"][0], wl["w_in"], tm)

    ff = z[:, C_FF:C_FF + N_HEADS].T.reshape(N_HEADS, length // LANES, LANES)
    logf, c = logf_cumsum(ff, wl["b_fox"], valid=LANES, chain=True)
    logf = logf.reshape(N_HEADS, length)
    c3 = _split3_host(c.reshape(N_HEADS, length) * LOG2E)
    c3_rows = [p[:, None, :] for p in c3]
    c3_cols = [-p[:, :, None] for p in c3]
    ones_rows = jnp.ones((N_HEADS, 3, length), BF16)
    ones_cols = jnp.ones((N_HEADS, length, 3), BF16)
    t_a = _pick(length, (512, 256, 128))
    q_a = _pad_rows([_heads_t(z[:, C_FQ:C_FQ + GROUP_W] * (SCALE * LOG2E))] + c3_rows + [ones_rows], LANES, BF16)
    k_rows = jnp.concatenate(
        [_heads_t(z[:, C_FK:C_FK + GROUP_W]).transpose(0, 2, 1).astype(BF16), ones_cols] + c3_cols
        + [jnp.zeros((N_HEADS, length, LANES - HEAD_DIM - 6), BF16)], axis=-1)
    v_t = _pad_rows([_heads_t(z[:, C_FV:C_FV + GROUP_W]), jnp.ones((N_HEADS, 1, length), BF16)], LANES, BF16)
    o_a = flash_attention(q_a, *_kv_tiles(k_rows, v_t, t_a), tq=t_a, hstack=1)
    o_a = o_a.transpose(2, 0, 1).reshape(length, GROUP_W)

    chunk = 64
    n_chunks = _pick(length // chunk, (8, 4, 2, 1))
    s0t = jnp.zeros((1, N_HEADS, HEAD_DIM, HEAD_DIM), F32)
    o_b, st = hgrn(z[None], hgrn_lower, wl["gm"][GROUP_W:2 * GROUP_W], s0t, layer=l, chunk=chunk,
                   n_chunks=n_chunks, valid_len=length)
    o_b = o_b[0]
    s_fin = st.transpose(0, 1, 3, 2)

    tq = 128
    kvcmp = nsa_compress(z, wl["w2cmp"], _pick(length, (1024, 512, 256, 128)))
    ns = length // NSA_SEL_BLOCK
    nq = z[:, C_NQ:C_NQ + GROUP_W]
    o_c, sel_t, _ = cmp_topk(nq.T[None], _evens_odds_cols(kvcmp.T)[None], tq=tq, pos0=0,
                             n_rounds=min(NSA_TOPK, ns))
    o_c = o_c[0].T
    n_win = -(-ns // SEL_WINDOW)
    sel_t = jnp.pad(sel_t, ((0, 0), (0, n_win * SEL_WINDOW - ns), (0, 0)))
    q_n = _pad_rows([_q_cols_tiles(nq.reshape(length, N_HEADS, HEAD_DIM) * (SCALE * LOG2E), tq)], LANES, BF16)
    nkv = z[:, C_NKV:C_NKV + 6 * HEAD_DIM]
    onehot = ((jnp.arange(length)[:, None] // NSA_SEL_BLOCK) % SEL_WINDOW == jnp.arange(SEL_WINDOW)[None, :])
    z64 = jnp.zeros((length, HEAD_DIM), BF16)
    one_row = jnp.ones((1, length), BF16)
    k_s = jnp.concatenate([nkv[:, 128:192].astype(BF16), z64, onehot.astype(BF16)], axis=-1)
    v_s = _pad_rows([nkv[:, 192:256].T, one_row], LANES, BF16)
    tk_s = _pick(length, (512, 256, 128))
    o_s = flash_attention(q_n[None], *_kv_tiles(k_s[None], v_s[None], tk_s), sel_t, tq=tq, hstack=N_HEADS)
    o_s = _from_cols_tiles(o_s[0], tq)
    k_w = jnp.concatenate([nkv[:, 256:320].astype(BF16), z64], axis=-1)
    v_w = _pad_rows([nkv[:, 320:384].T, one_row], LANES, BF16)
    o_w = flash_attention(q_n[None], *_kv_tiles(k_w[None], v_w[None], tq), tq=tq, hstack=N_HEADS,
                          window=NSA_WINDOW)
    o_w = _from_cols_tiles(o_w[0], tq)

    conv0 = jnp.zeros((1, LRU_PAD, GROUP_W), F32)
    h0 = jnp.zeros((1, 1, GROUP_W), F32)
    o_d, h_last = rglru(z[None], conv0, h0, wl["cw"], wl["cb"], wl["wa"], wl["ba"], wl["wx"], wl["bx"],
                        wl["lam"], tb=_pick(length, (512, 256, 128)))

    y = mix(x, o_a, o_b, o_c, o_s, o_w, o_d[0], z, wl["gm"], wl["w_out"], wl["n"][1], tm)
    xa_kv = rms_matmul(mem, wl["g_mem"], wl["wkv"], mem.shape[0])
    y = xattn(y[None], wl["n"][2], wl["wq"], xa_kv[None], wl["wo"], wl["n"][3], tm)[0]
    y = ffn(y, wl["n"][4], wl["w1"], wl["w2"], wl["n"][5], _pick(length, (256, 128)))

    wb = min(NSA_WINDOW, length)
    state = (
        z[:, C_FK:C_FK + 2 * GROUP_W].reshape(1, length, 2, N_HEADS, HEAD_DIM),
        logf.T[None],
        nkv[:, :4 * HEAD_DIM].reshape(1, length, 4, HEAD_DIM),
        nkv[length - wb:, 4 * HEAD_DIM:].reshape(1, wb, 2, HEAD_DIM),
        s_fin,
        h_last[:, 0],
        z[length - (LRU_CONV - 1):, C_UX:C_UX + GROUP_W][None],
        xa_kv.reshape(1, mem.shape[0], 2, N_HEADS, HEAD_DIM),
    )
    return y, state


def _sample_layer(x, wl, l, hgrn_lower, page_table, cache_fox_kvt, cache_fox_logft, cache_nsat, win_buf,
                  s0, h0, conv_buf, xa_kv):
    bsz, n_tok, _ = x.shape
    rows = bsz * n_tok
    n_pages = page_table.shape[1]
    past = n_pages * PAGE
    z = rms_matmul(x.reshape(rows, D_MODEL), wl["n"][0], wl["w_in"], rows)
    zb = z.reshape(bsz, n_tok, D_PROJ_PAD)

    ff = jnp.pad(zb[:, :, C_FF:C_FF + N_HEADS].transpose(2, 0, 1), ((0, 0), (0, 0), (0, LANES - n_tok)))
    logf, cn = logf_cumsum(ff, wl["b_fox"], valid=n_tok, chain=False)
    logf_new = logf[:, :, :n_tok].transpose(1, 2, 0)
    cn_t = cn[:, :, :n_tok].transpose(1, 0, 2)
    cnb = jnp.broadcast_to(cn_t.reshape(bsz, N_HEADS * n_tok, 1), (bsz, N_HEADS * n_tok, LANES))
    cnk = jnp.pad(jnp.repeat(cn_t, n_tok, axis=1), ((0, 0), (0, 0), (0, LANES - n_tok)))
    fq = zb[:, :, C_FQ:C_FQ + GROUP_W] * SCALE
    head_mask = (jnp.arange(N_HEADS)[:, None] == jnp.arange(GROUP_W)[None, :] // HEAD_DIM)
    qbd = jnp.where(head_mask[None, :, None, :], fq[:, None], 0.0).reshape(bsz, N_HEADS * n_tok, GROUP_W)
    pad_tok = ((0, 0), (0, 8 - n_tok), (0, 0))
    knew = jnp.pad(zb[:, :, C_FK:C_FK + GROUP_W], pad_tok).astype(BF16)
    vnew = jnp.pad(zb[:, :, C_FV:C_FV + GROUP_W], pad_tok).astype(BF16)
    o_a = fox_sample(page_table, qbd.astype(BF16), cnb, cnk, knew, vnew, cache_fox_kvt, cache_fox_logft,
                     layer=l, n_tok=n_tok)

    chunk = 8
    zp = jnp.pad(zb, ((0, 0), (0, chunk - n_tok), (0, 0)))
    o_b, st = hgrn(zp, hgrn_lower, wl["gm"][GROUP_W:2 * GROUP_W], s0.transpose(0, 1, 3, 2), layer=l,
                   chunk=chunk, n_chunks=1, valid_len=n_tok)
    o_b = o_b[:, :n_tok]
    s_fin = st.transpose(0, 1, 3, 2)

    nkv = zb[:, :, C_NKV:C_NKV + 6 * HEAD_DIM]
    nq = zb[:, :, C_NQ:C_NQ + GROUP_W]
    kvcmp = nsa_compress_pages(page_table, wl["wtcmp"], cache_nsat, layer=l)
    kvcmp = kvcmp.transpose(0, 2, 1, 3).reshape(bsz, 2 * HEAD_DIM, past // NSA_CMP_BLOCK)
    n_in = min(NSA_TOPK, past // NSA_SEL_BLOCK + 1) - 1
    o_c, _, idx = cmp_topk(nq.transpose(0, 2, 1), _evens_odds_cols(kvcmp), tq=n_tok, pos0=past, n_rounds=n_in)
    o_c = o_c.transpose(0, 2, 1)
    idx = idx.transpose(0, 2, 1).reshape(rows, NSA_TOPK)
    q_rows = jnp.pad(nq.reshape(rows, N_HEADS, HEAD_DIM), ((0, 0), (0, 8 - N_HEADS), (0, 0))).astype(BF16)
    ks_new = jnp.pad(nkv[:, :, 128:192], pad_tok).astype(BF16)
    vs_new = jnp.pad(nkv[:, :, 192:256], pad_tok).astype(BF16)
    o_s = slc_sample(idx, page_table, q_rows, ks_new, vs_new, cache_nsat, layer=l, n_sel=n_in, n_tok=n_tok)
    o_s = o_s[:, :N_HEADS].reshape(bsz, n_tok, GROUP_W)
    n_win = win_buf.shape[1]
    win = jnp.concatenate([win_buf, nkv[:, :, 256:384].reshape(bsz, n_tok, 2, HEAD_DIM)], axis=1)
    n_keys = n_win + n_tok
    kpad = -(-n_keys // 8) * 8
    win_p = jnp.pad(win, ((0, 0), (0, kpad - n_keys), (0, 0), (0, 0))).astype(BF16)
    q_ht = nq.reshape(bsz, n_tok, N_HEADS, HEAD_DIM).transpose(0, 2, 1, 3).reshape(bsz, N_HEADS * n_tok, HEAD_DIM)
    o_w = win_sample(q_ht.astype(BF16), win_p[:, :, 0], win_p[:, :, 1], n_tok=n_tok, n_win=n_win, n_keys=n_keys)
    o_w = o_w.reshape(bsz, N_HEADS, n_tok, HEAD_DIM).transpose(0, 2, 1, 3).reshape(bsz, n_tok, GROUP_W)

    conv0 = jnp.pad(conv_buf, ((0, 0), (LRU_PAD - (LRU_CONV - 1), 0), (0, 0)))
    o_d, h_last = rglru(zb, conv0, h0[:, None], wl["cw"], wl["cb"], wl["wa"], wl["ba"], wl["wx"], wl["bx"],
                        wl["lam"], tb=n_tok)

    flat = lambda a: a.reshape(rows, GROUP_W)
    y = mix(x.reshape(rows, D_MODEL), flat(o_a), flat(o_b), flat(o_c), flat(o_s), flat(o_w), flat(o_d), z,
            wl["gm"], wl["w_out"], wl["n"][1], rows)
    mem_len = xa_kv.shape[1]
    y = xattn(y.reshape(bsz, n_tok, D_MODEL), wl["n"][2], wl["wq"], xa_kv.reshape(bsz, mem_len, 2 * GROUP_W),
              wl["wo"], wl["n"][3], n_tok)
    y = ffn(y.reshape(rows, D_MODEL), wl["n"][4], wl["w1"], wl["w2"], wl["n"][5], rows)

    state = (
        zb[:, :, C_FK:C_FK + 2 * GROUP_W].reshape(bsz, n_tok, 2, N_HEADS, HEAD_DIM),
        logf_new,
        nkv[:, :, :4 * HEAD_DIM].reshape(bsz, n_tok, 4, HEAD_DIM),
        win[:, n_tok:],
        s_fin,
        h_last[:, 0],
        jnp.concatenate([conv_buf, zb[:, :, C_UX:C_UX + GROUP_W]], axis=1)[:, n_tok:],
    )
    return y.reshape(bsz, n_tok, D_MODEL), state


def kernel(x_prompt, mem_prompt, x_sample, cache_fox_kv, cache_fox_logf, cache_nsa_kv, state_nsa_win, state_hgrn,
           state_lru_h, state_lru_conv, cache_xa_kv, page_table, w_in, b_fox_f, hgrn_lower, nsa_cmp_w, lru_conv_w,
           lru_conv_b, lru_wa, lru_ba, lru_wx, lru_bx, lru_lambda, g_mix, w_out, g_mem, xa_wq, xa_wkv, xa_wo,
           w_ff1, w_ff2, norms):
    assert x_prompt.shape[0] == 1 and mem_prompt.shape[0] == 1
    w = dict(w_in=w_in, b_fox_f=b_fox_f, nsa_cmp_w=nsa_cmp_w, lru_conv_w=lru_conv_w, lru_conv_b=lru_conv_b,
             lru_wa=lru_wa, lru_ba=lru_ba, lru_wx=lru_wx, lru_bx=lru_bx, lru_lambda=lru_lambda, g_mix=g_mix,
             w_out=w_out, g_mem=g_mem, xa_wq=xa_wq, xa_wkv=xa_wkv, xa_wo=xa_wo, w_ff1=w_ff1, w_ff2=w_ff2,
             norms=norms)
    depth = w_in.shape[0]
    fox_kvt = cache_fox_kv.transpose(0, 1, 3, 4, 5, 2)
    fox_logft = cache_fox_logf.transpose(0, 1, 3, 2)
    nsat = cache_nsa_kv.transpose(0, 1, 3, 4, 2)
    y_p, y_s = x_prompt[0], x_sample
    p_out, s_out = [], []
    for l in range(depth):
        wl = _layer_weights(w, l)
        y_p, po = _prompt_layer(y_p, mem_prompt[0], wl, l, hgrn_lower)
        y_s, so = _sample_layer(y_s, wl, l, hgrn_lower, page_table, fox_kvt, fox_logft, nsat,
                                state_nsa_win[l], state_hgrn[l], state_lru_h[l], state_lru_conv[l], cache_xa_kv[l])
        p_out.append(po)
        s_out.append(so)
    stack = lambda outs, i: jnp.stack([o[i] for o in outs], axis=0)
    return ((y_p[None], y_s) + tuple(stack(p_out, i) for i in range(8)) + tuple(stack(s_out, i) for i in range(7)))
```

```python
import functools

import jax
import jax.numpy as jnp
import numpy as np
from jax import lax
from jax.experimental import pallas as pl
from jax.experimental.pallas import tpu as pltpu

F32 = jnp.float32
BF16 = jnp.bfloat16
I32 = jnp.int32

D_MODEL = 1024
GROUP_W = 256
HEAD_DIM = 64
N_HEADS = 4
NSA_CMP_BLOCK = 32
NSA_SEL_BLOCK = 64
NSA_TOPK = 16
NSA_WINDOW = 512
LRU_CONV = 4
LRU_C = 8.0
D_FF = 4096
EPS = 1e-6
NEG_BIG = -1e30
LB_FLOOR = 1e-30
SCALE = HEAD_DIM ** -0.5
LOG2E = float(np.log2(np.e))
PAGE = 128

V7X_VMEM_BYTES = 64 * 1024 * 1024
VMEM_LIMIT = V7X_VMEM_BYTES - 8 * 1024 * 1024
LANES = 128

C_FQ, C_FK, C_FV = 0, 256, 512
C_HQ, C_HF, C_HI, C_HG = 768, 1024, 1280, 1536
C_NQ = 1792
C_UX, C_UG = 2048, 2304
C_NKV = 2560
C_FF, C_NG = 2944, 2948
D_PROJ_PAD = 3072


def _params(*sem):
    return pltpu.CompilerParams(dimension_semantics=sem, vmem_limit_bytes=VMEM_LIMIT)


def _const_spec(shape):
    n = len(shape)
    return pl.BlockSpec(shape, lambda *_: (0,) * n)


def _rms(x, g):
    return x * lax.rsqrt(jnp.mean(x * x, axis=-1, keepdims=True) + EPS) * g


def _sigmoid(x):
    return 1.0 / (1.0 + jnp.exp(-x))


def _softplus(x):
    return jnp.maximum(x, 0.0) + jnp.log1p(jnp.exp(-jnp.abs(x)))


def _log_sigmoid(x):
    return -_softplus(-x)


def _gelu_tanh(x):
    return 0.5 * x * (1.0 + jnp.tanh(np.sqrt(2.0 / np.pi) * (x + 0.044715 * (x * x * x))))


def _trunc_bf16(x):
    bits = lax.bitcast_convert_type(x, jnp.uint32) & jnp.uint32(0xFFFF0000)
    return lax.bitcast_convert_type(bits, F32)


def _split3(x):
    hi = _trunc_bf16(x)
    r = x - hi
    mid = _trunc_bf16(r)
    lo = _trunc_bf16(r - mid)
    return hi.astype(BF16), mid.astype(BF16), lo.astype(BF16)


def _dot(a, b):
    return jnp.dot(a, b, preferred_element_type=F32)


def _dot_nt(a, b):
    return lax.dot_general(a, b, (((1,), (1,)), ((), ())), preferred_element_type=F32)


def _dot_tn(a, b):
    return lax.dot_general(a, b, (((0,), (0,)), ((), ())), preferred_element_type=F32)


def _dot_exact_lhs(x, m01):
    hi, mid, lo = _split3(x)
    return _dot(hi, m01) + _dot(mid, m01) + _dot(lo, m01)


def _dot_exact_rhs(m01, x):
    hi, mid, lo = _split3(x)
    return _dot(m01, hi) + _dot(m01, mid) + _dot(m01, lo)


def _rms_matmul_kernel(x_ref, g_ref, w_ref, o_ref):
    h = _rms(x_ref[...], g_ref[...])
    o_ref[...] = _dot(h.astype(BF16), w_ref[...])


def rms_matmul(x, g, w, tm):
    m, k = x.shape
    n = w.shape[1]
    return pl.pallas_call(
        _rms_matmul_kernel,
        grid=(m // tm,),
        in_specs=[pl.BlockSpec((tm, k), lambda i: (i, 0)), _const_spec((1, k)), _const_spec((k, n))],
        out_specs=pl.BlockSpec((tm, n), lambda i: (i, 0)),
        out_shape=jax.ShapeDtypeStruct((m, n), F32),
        compiler_params=_params("parallel"),
        name="rms_matmul",
    )(x, g.reshape(1, k), w)


def _logf_kernel(b_ref, x_ref, u_ref, ls_ref, logf_ref, c_ref, *, valid, chain):
    h = pl.program_id(0)
    lf = _log_sigmoid(x_ref[0] + b_ref[h])
    n = lf.shape[0]
    if valid < LANES:
        lane = lax.broadcasted_iota(I32, (n, LANES), 1)
        lf = jnp.where(lane < valid, lf, 0.0)
    logf_ref[0] = lf
    c = _dot_exact_lhs(lf, u_ref[...])
    if chain:
        tot = jnp.broadcast_to(c[:, LANES - 1:LANES], (n, LANES))
        c = c + _dot_exact_rhs(ls_ref[...], tot)
    c_ref[0] = c


def logf_cumsum(ff, bias, valid, chain):
    nh, n, _ = ff.shape
    u = jnp.triu(jnp.ones((LANES, LANES), F32)).astype(BF16)
    ls = jnp.tril(jnp.ones((n, n), F32), -1).astype(BF16)
    blk = pl.BlockSpec((1, n, LANES), lambda h: (h, 0, 0))
    return pl.pallas_call(
        functools.partial(_logf_kernel, valid=valid, chain=chain),
        grid=(nh,),
        in_specs=[pl.BlockSpec(memory_space=pltpu.SMEM), blk, _const_spec((LANES, LANES)), _const_spec((n, n))],
        out_specs=[blk, blk],
        out_shape=[jax.ShapeDtypeStruct(ff.shape, F32)] * 2,
        compiler_params=_params("parallel"),
        name="logf_cumsum",
    )(bias, ff, u, ls)


SEL_WINDOW = LANES


def _flash_kernel(*refs, tq, tk, hstack, window, with_sel):
    if with_sel:
        q_ref, sel_ref, k_ref, v_ref, o_ref, m_sc, acc_sc = refs
    else:
        q_ref, k_ref, v_ref, o_ref, m_sc, acc_sc = refs
    i = pl.program_id(1)
    cols = hstack * tq
    n_kv = k_ref.shape[1]
    q = q_ref[0]
    m_sc[...] = jnp.full((1, cols), NEG_BIG, F32)
    acc_sc[...] = jnp.zeros((LANES, cols), F32)
    qs = i * tq
    jd = qs // tk
    qpos = qs + lax.broadcasted_iota(I32, (1, cols), 1) % tq

    def softmax_pv(j, s, masked):
        if masked:
            kpos = j * tk + lax.broadcasted_iota(I32, (tk, 1), 0)
            valid = kpos <= qpos
            if window is not None:
                valid = jnp.logical_and(valid, qpos - kpos < window)
            s = jnp.where(valid, s, NEG_BIG)
        m_old = m_sc[...]
        m_new = jnp.maximum(m_old, jnp.max(s, axis=0, keepdims=True))
        p = jnp.exp2(s - m_new)
        if masked:
            p = jnp.where(valid, p, 0.0)
        alpha = jnp.exp2(m_old - m_new)
        acc_sc[...] = alpha * acc_sc[...] + _dot(v_ref[0, j], p.astype(BF16))
        m_sc[...] = m_new

    def step(j, qw, masked):
        softmax_pv(j, _dot(k_ref[0, j], qw), masked)

    def loop(lo, hi, qw, masked):
        n = hi - lo

        def body(pair, carry):
            j = lo + 2 * pair
            step(j, qw, masked)
            step(j + 1, qw, masked)
            return carry
        lax.fori_loop(0, n // 2, body, 0)

        @pl.when(n % 2 == 1)
        def _():
            step(hi - 1, qw, masked)

    n_diag = max(1, tq // tk)
    if with_sel:
        assert n_diag == 1
        per_win = SEL_WINDOW * NSA_SEL_BLOCK // tk
        for w in range(-(-n_kv // per_win)):
            lo_w, hi_w = w * per_win, min((w + 1) * per_win, n_kv)
            sel = sel_ref[0, w * SEL_WINDOW:(w + 1) * SEL_WINDOW, :]
            qw = jnp.concatenate([q, jnp.concatenate([sel] * hstack, axis=1)], axis=0)
            loop(lo_w, jnp.clip(jd, lo_w, hi_w), qw, False)

            @pl.when(jnp.logical_and(jd >= lo_w, jd < hi_w))
            def _():
                step(jd, qw, True)
    elif window is None:
        loop(0, jd, q, False)
        for d in range(n_diag):
            step(jd + d, q, True)
    else:
        loop(jnp.maximum(jd - window // tk, 0), jd + n_diag, q, True)
    acc = acc_sc[...]
    den = acc[HEAD_DIM:HEAD_DIM + 1, :]
    o_ref[0] = acc[0:HEAD_DIM, :] / jnp.where(den > 0.0, den, 1.0)


def flash_attention(qt, k, vt, sel=None, *, tq, hstack, window=None):
    g, kq, qcols = qt.shape
    gk, n_kv, tk, kd = k.shape
    cols = hstack * tq
    kmap = (lambda a, i: (a, 0, 0, 0)) if gk > 1 else (lambda a, i: (0, 0, 0, 0))
    in_specs = [pl.BlockSpec((1, kq, cols), lambda a, i: (a, 0, i))]
    args = [qt]
    if sel is not None:
        in_specs.append(pl.BlockSpec((1, sel.shape[1], tq), lambda a, i: (a, 0, i)))
        args.append(sel)
    in_specs += [pl.BlockSpec((1, n_kv, tk, kd), kmap), pl.BlockSpec((1, n_kv, LANES, tk), kmap)]
    return pl.pallas_call(
        functools.partial(_flash_kernel, tq=tq, tk=tk, hstack=hstack, window=window, with_sel=sel is not None),
        grid=(g, qcols // cols),
        in_specs=in_specs,
        out_specs=pl.BlockSpec((1, HEAD_DIM, cols), lambda a, i: (a, 0, i)),
        out_shape=jax.ShapeDtypeStruct((g, HEAD_DIM, qcols), F32),
        scratch_shapes=[pltpu.VMEM((1, cols), F32), pltpu.VMEM((LANES, cols), F32)],
        compiler_params=_params("parallel", "arbitrary"),
        name="flash_attention",
    )(*args, k, vt)


def _compress_kernel(kv_ref, w_ref, o_ref):
    kv = kv_ref[...]
    n = kv.shape[0] // NSA_CMP_BLOCK
    blocks = kv.reshape(n, NSA_CMP_BLOCK, LANES) * w_ref[...][None]
    o_ref[...] = jnp.sum(blocks, axis=1)


def nsa_compress(z, w2, tl):
    length = z.shape[0]
    return pl.pallas_call(
        _compress_kernel,
        grid=(length // tl,),
        in_specs=[pl.BlockSpec((tl, LANES), lambda i: (i, C_NKV // LANES)), _const_spec((NSA_CMP_BLOCK, LANES))],
        out_specs=pl.BlockSpec((tl // NSA_CMP_BLOCK, LANES), lambda i: (i, 0)),
        out_shape=jax.ShapeDtypeStruct((length // NSA_CMP_BLOCK, LANES), F32),
        compiler_params=_params("parallel"),
        name="nsa_compress",
    )(z, w2)


def _cmp_topk_kernel(q_ref, kv_ref, oc_ref, sel_ref, idx_ref, *, tq, pos0, n_rounds):
    i = pl.program_id(1)
    qt = q_ref[0]
    kv = kv_ref[0]
    n = kv.shape[1]
    nsc = n // 2
    kc = jnp.transpose(kv[:HEAD_DIM]).astype(BF16)
    vc = kv[HEAD_DIM:].astype(BF16)
    pos = pos0 + i * tq + lax.broadcasted_iota(I32, (1, tq), 1)
    slot = lax.broadcasted_iota(I32, (n, 1), 0)
    cmp_idx = jnp.where(slot < nsc, 2 * slot, 2 * (slot - nsc) + 1)
    valid = (cmp_idx + 1) * NSA_CMP_BLOCK - 1 <= pos
    imp = jnp.zeros((n, tq), F32)
    for h in range(N_HEADS):
        rows = slice(h * HEAD_DIM, (h + 1) * HEAD_DIM)
        s = jnp.where(valid, _dot(kc, qt[rows].astype(BF16)) * SCALE, NEG_BIG)
        m = jnp.max(s, axis=0, keepdims=True)
        e = jnp.where(valid, jnp.exp(s - m), 0.0)
        d = jnp.sum(e, axis=0, keepdims=True)
        p = e / jnp.where(d > 0.0, d, 1.0)
        oc_ref[0, rows, :] = _dot(vc, p.astype(BF16))
        imp = imp + p

    imp = imp[:nsc] + imp[nsc:]
    blk = lax.broadcasted_iota(I32, (nsc, 1), 0)
    blk_f = blk.astype(F32)
    forced = jnp.logical_or(blk == pos // NSA_SEL_BLOCK, blk == 0)
    causal = blk * NSA_SEL_BLOCK <= pos
    imp = jnp.where(forced, 1e9, jnp.where(causal, imp, -1e9))
    sel = jnp.zeros((nsc, tq), F32)
    rnd = lax.broadcasted_iota(I32, (idx_ref.shape[1], tq), 0)
    idx = jnp.full(rnd.shape, -1, I32)
    for r in range(n_rounds):
        mx = jnp.max(imp, axis=0, keepdims=True)
        first = jnp.min(jnp.where(imp == mx, blk_f, float(nsc)), axis=0, keepdims=True)
        hit = blk_f == first
        ok = mx >= 0.0
        sel = jnp.where(jnp.logical_and(hit, ok), 1.0, sel)
        idx = jnp.where(rnd == r, jnp.where(ok, first.astype(I32), -1), idx)
        imp = jnp.where(hit, -3e38, imp)
    sel_ref[0] = jnp.where(sel > 0.0, 0.0, NEG_BIG).astype(BF16)
    idx_ref[0] = idx


def cmp_topk(qt, kvt, *, tq, pos0, n_rounds):
    g, _, lq = qt.shape
    gk, _, n = kvt.shape
    nsc = n // 2
    kmap = (lambda a, i: (a, 0, 0)) if gk > 1 else (lambda a, i: (0, 0, 0))
    col = lambda r: pl.BlockSpec((1, r, tq), lambda a, i: (a, 0, i))
    return pl.pallas_call(
        functools.partial(_cmp_topk_kernel, tq=tq, pos0=pos0, n_rounds=n_rounds),
        grid=(g, lq // tq),
        in_specs=[col(GROUP_W), pl.BlockSpec((1, LANES, n), kmap)],
        out_specs=[col(GROUP_W), col(nsc), col(NSA_TOPK)],
        out_shape=[jax.ShapeDtypeStruct((g, GROUP_W, lq), F32), jax.ShapeDtypeStruct((g, nsc, lq), BF16),
                   jax.ShapeDtypeStruct((g, NSA_TOPK, lq), I32)],
        compiler_params=_params("parallel", "parallel"),
        name="cmp_topk",
    )(qt, kvt)


HG_SUB = 16
HG_PAD = 16
HG_PAIR = 2 * HEAD_DIM


def _hgrn_kernel(q_ref, f_ref, i_ref, g_ref, low_ref, gain_ref, s0_ref, tril_ref, ones_ref,
                 o_ref, s_out_ref, st_sc, ksh_sc, bsh_sc, vsh_sc, *, layer, chunk, n_chunks, valid_len):
    step = pl.program_id(1)
    sub = min(HG_SUB, chunk)

    @pl.when(step == 0)
    def _():
        st_sc[...] = s0_ref[0]
        zeros = jnp.zeros((HG_PAD, HG_PAIR), F32)
        ksh_sc[0:HG_PAD, :] = zeros
        bsh_sc[0:HG_PAD, :] = zeros
        vsh_sc[0:HG_PAD, :] = zeros

    low = low_ref[...]
    e = jnp.exp(low - jnp.max(low, axis=0, keepdims=True))
    lower = e / jnp.sum(e, axis=0, keepdims=True)
    cs = lower[0:1]
    for d in range(1, layer + 1):
        cs = cs + lower[d:d + 1]
    lb_all = cs - lower[0:1]
    tril = tril_ref[...]
    ones_bd = ones_ref[...]
    row = lax.broadcasted_iota(I32, (chunk, 1), 0)

    def do_chunk(c, carry):
        r0 = pl.multiple_of(c * chunk, chunk)
        base = step * (n_chunks * chunk) + r0
        live = (base + row) < valid_len
        outs = []
        for pair in range(N_HEADS // 2):
            pc = slice(pair * HG_PAIR, (pair + 1) * HG_PAIR)
            lb = lb_all[:, pc]
            zq = q_ref[0, pl.ds(r0, chunk), pc]
            zf = f_ref[0, pl.ds(r0, chunk), pc]
            v = i_ref[0, pl.ds(r0, chunk), pc]
            q = zq * _sigmoid(zq)
            a = jnp.log1p(-lb) + _log_sigmoid(zf)
            b0 = jnp.log(jnp.maximum(lb, LB_FLOOR))
            log_f = jnp.maximum(a, b0) + jnp.log1p(jnp.exp(-jnp.abs(a - b0)))
            k = (1.0 - lb) * _sigmoid(-zf)
            log_f = jnp.where(live, log_f, 0.0)
            k = jnp.where(live, k, 0.0)
            b = _dot_exact_rhs(tril, log_f)
            ksh_sc[HG_PAD:HG_PAD + chunk, :] = k
            bsh_sc[HG_PAD:HG_PAD + chunk, :] = b
            vsh_sc[HG_PAD:HG_PAD + chunk, :] = v
            o2 = jnp.zeros((chunk, HG_PAIR), F32)
            for d in range(sub):
                kd = ksh_sc[HG_PAD - d:HG_PAD - d + chunk, :]
                bd = bsh_sc[HG_PAD - d:HG_PAD - d + chunk, :]
                vd = vsh_sc[HG_PAD - d:HG_PAD - d + chunk, :]
                ok = (row % sub) >= d
                w = jnp.where(ok, q * kd * jnp.exp(jnp.where(ok, b - bd, 0.0)), 0.0)
                o2 = o2 + _dot(w.astype(BF16), ones_bd) * vd
            qe = q * jnp.exp(b)
            b_last = b[chunk - 1:chunk]
            kl = k * jnp.exp(b_last - b)
            e_last = jnp.exp(b_last)
            for hh in range(2):
                h = 2 * pair + hh
                hc = slice(hh * HEAD_DIM, (hh + 1) * HEAD_DIM)
                st = st_sc[h]
                qh, kh, bh, vh = q[:, hc], k[:, hc], b[:, hc], v[:, hc]
                o = o2[:, hc] + _dot_nt(qe[:, hc].astype(BF16), st.astype(BF16))
                parts = [o[0:sub]]
                for s_i in range(1, chunk // sub):
                    lo = s_i * sub
                    b_start = bh[lo - 1:lo]
                    qd = qh[lo:lo + sub] * jnp.exp(bh[lo:lo + sub] - b_start)
                    kdec = kh[0:lo] * jnp.exp(b_start - bh[0:lo])
                    att = _dot_nt(qd.astype(BF16), kdec.astype(BF16))
                    parts.append(o[lo:lo + sub] + _dot(att.astype(BF16), vh[0:lo].astype(BF16)))
                o = jnp.concatenate(parts, axis=0) if len(parts) > 1 else parts[0]
                st_sc[h] = e_last[:, hc] * st + _dot_tn(vh.astype(BF16), kl[:, hc].astype(BF16))
                outs.append(_rms(o, gain_ref[:, h * HEAD_DIM:(h + 1) * HEAD_DIM]))
        zg = g_ref[0, pl.ds(r0, chunk), :]
        o_ref[0, pl.ds(r0, chunk), :] = jnp.concatenate(outs, axis=-1) * (zg * _sigmoid(zg))
        return carry

    lax.fori_loop(0, n_chunks, do_chunk, 0)

    @pl.when(step == pl.num_programs(1) - 1)
    def _():
        s_out_ref[0] = st_sc[...]


def hgrn(z, hgrn_lower, gain, s0t, *, layer, chunk, n_chunks, valid_len):
    bsz, lp, _ = z.shape
    tb = chunk * n_chunks
    tril = jnp.tril(jnp.ones((chunk, chunk), F32)).astype(BF16)
    lane_head = np.arange(HG_PAIR) // HEAD_DIM
    ones_bd = jnp.asarray(lane_head[:, None] == lane_head[None, :], BF16)
    col = lambda c: pl.BlockSpec((1, tb, GROUP_W), lambda b, i: (b, i, c // GROUP_W))
    st_spec = pl.BlockSpec((1, N_HEADS, HEAD_DIM, HEAD_DIM), lambda b, i: (b, 0, 0, 0))
    depth = hgrn_lower.shape[0]
    return pl.pallas_call(
        functools.partial(_hgrn_kernel, layer=layer, chunk=chunk, n_chunks=n_chunks, valid_len=valid_len),
        grid=(bsz, lp // tb),
        in_specs=[col(C_HQ), col(C_HF), col(C_HI), col(C_HG), _const_spec((depth, GROUP_W)),
                  _const_spec((1, GROUP_W)), st_spec, _const_spec((chunk, chunk)), _const_spec((HG_PAIR, HG_PAIR))],
        out_specs=[pl.BlockSpec((1, tb, GROUP_W), lambda b, i: (b, i, 0)), st_spec],
        out_shape=[jax.ShapeDtypeStruct((bsz, lp, GROUP_W), F32),
                   jax.ShapeDtypeStruct((bsz, N_HEADS, HEAD_DIM, HEAD_DIM), F32)],
        scratch_shapes=[pltpu.VMEM((N_HEADS, HEAD_DIM, HEAD_DIM), F32)]
                       + [pltpu.VMEM((HG_PAD + chunk, HG_PAIR), F32)] * 3,
        compiler_params=_params("parallel", "arbitrary"),
        name="hgrn2",
    )(z, z, z, z, hgrn_lower, gain.reshape(1, GROUP_W), s0t, tril, ones_bd)


LRU_PAD = 8


def _rglru_kernel(ux_ref, ug_ref, conv0_ref, h0_ref, cw_ref, cb_ref, wa_ref, ba_ref, wx_ref, bx_ref,
                  lam_ref, y_ref, hl_ref, xp_sc, a_sc, b_sc, h_sc, hc_sc, *, tb):
    step = pl.program_id(1)

    @pl.when(step == 0)
    def _():
        xp_sc[0:LRU_PAD, :] = conv0_ref[0]
        hc_sc[...] = h0_ref[0]

    x = ux_ref[0]
    xp_sc[LRU_PAD:LRU_PAD + tb, :] = x
    cw = cw_ref[...]
    u = cb_ref[...] + xp_sc[LRU_PAD - 3:LRU_PAD - 3 + tb, :] * cw[0:1]
    for j in range(1, LRU_CONV):
        u = u + xp_sc[LRU_PAD - 3 + j:LRU_PAD - 3 + j + tb, :] * cw[j:j + 1]
    if tb >= LRU_PAD:
        xp_sc[0:LRU_PAD, :] = xp_sc[tb:tb + LRU_PAD, :]
    ub = u.astype(BF16)
    r = _sigmoid(_dot(ub, wa_ref[...]) + ba_ref[...])
    gi = _sigmoid(_dot(ub, wx_ref[...]) + bx_ref[...])
    log_a = -LRU_C * r * _softplus(-lam_ref[...])
    a = jnp.exp(log_a)
    a_sc[...] = a
    b_sc[...] = jnp.sqrt(1.0 - a * a) * (gi * u)

    def body(t, h):
        h = a_sc[pl.ds(t, 1), :] * h + b_sc[pl.ds(t, 1), :]
        h_sc[pl.ds(t, 1), :] = h
        return h

    h = lax.fori_loop(0, tb, body, hc_sc[...], unroll=min(tb, 8))
    hc_sc[...] = h
    y_ref[0] = h_sc[...] * _gelu_tanh(ug_ref[0])
    hl_ref[0] = h


def rglru(z, conv0, h0, cw, cb, wa_bd, ba, wx_bd, bx, lam, *, tb):
    bsz, length, _ = z.shape
    col = lambda c: pl.BlockSpec((1, tb, GROUP_W), lambda b, i: (b, i, c // GROUP_W))
    per_b = lambda r: pl.BlockSpec((1, r, GROUP_W), lambda b, i: (b, 0, 0))
    vec = _const_spec((1, GROUP_W))
    mat = _const_spec((GROUP_W, GROUP_W))
    return pl.pallas_call(
        functools.partial(_rglru_kernel, tb=tb),
        grid=(bsz, length // tb),
        in_specs=[col(C_UX), col(C_UG), per_b(LRU_PAD), per_b(1), _const_spec((LRU_CONV, GROUP_W)), vec,
                  mat, vec, mat, vec, vec],
        out_specs=[pl.BlockSpec((1, tb, GROUP_W), lambda b, i: (b, i, 0)), per_b(1)],
        out_shape=[jax.ShapeDtypeStruct((bsz, length, GROUP_W), F32), jax.ShapeDtypeStruct((bsz, 1, GROUP_W), F32)],
        scratch_shapes=[pltpu.VMEM((LRU_PAD + tb + LRU_PAD, GROUP_W), F32), pltpu.VMEM((tb, GROUP_W), F32),
                        pltpu.VMEM((tb, GROUP_W), F32), pltpu.VMEM((tb, GROUP_W), F32),
                        pltpu.VMEM((1, GROUP_W), F32)],
        compiler_params=_params("parallel", "arbitrary"),
        name="rglru",
    )(z, z, conv0, h0, cw, cb.reshape(1, -1), wa_bd, ba.reshape(1, -1), wx_bd, bx.reshape(1, -1),
      lam.reshape(1, -1))


def _mix_kernel(x_ref, oa_ref, ob_ref, oc_ref, os_ref, ow_ref, od_ref, zl_ref, gm_ref, w_ref, n_ref, y_ref):
    gm = gm_ref[...]
    gates = _sigmoid(zl_ref[...][:, C_NG - C_FF:C_NG - C_FF + 3 * N_HEADS])
    oc, osel, ow = oc_ref[...], os_ref[...], ow_ref[...]
    parts = []
    for h in range(N_HEADS):
        cols = slice(h * HEAD_DIM, (h + 1) * HEAD_DIM)
        parts.append(gates[:, 3 * h:3 * h + 1] * oc[:, cols] + gates[:, 3 * h + 1:3 * h + 2] * osel[:, cols]
                     + gates[:, 3 * h + 2:3 * h + 3] * ow[:, cols])
    o_nsa = jnp.concatenate(parts, axis=-1)
    groups = (_rms(oa_ref[...], gm[:, 0:GROUP_W]), ob_ref[...],
              _rms(o_nsa, gm[:, 2 * GROUP_W:3 * GROUP_W]), _rms(od_ref[...], gm[:, 3 * GROUP_W:]))
    y = jnp.zeros(x_ref.shape, F32)
    for gi, grp in enumerate(groups):
        y = y + _dot(grp.astype(BF16), w_ref[gi * GROUP_W:(gi + 1) * GROUP_W, :])
    y_ref[...] = x_ref[...] + _rms(y, n_ref[...])


def mix(x, oa, ob, oc, osel, ow, od, z, gm, w_out, n1, tm):
    m = x.shape[0]
    row = lambda w: pl.BlockSpec((tm, w), lambda i: (i, 0))
    return pl.pallas_call(
        _mix_kernel,
        grid=(m // tm,),
        in_specs=[row(D_MODEL)] + [row(GROUP_W)] * 6
                 + [pl.BlockSpec((tm, LANES), lambda i: (i, C_FF // LANES)), _const_spec((1, D_MODEL)),
                    _const_spec((D_MODEL, D_MODEL)), _const_spec((1, D_MODEL))],
        out_specs=row(D_MODEL),
        out_shape=jax.ShapeDtypeStruct((m, D_MODEL), F32),
        compiler_params=_params("parallel"),
        name="mix_residual",
    )(x, oa, ob, oc, osel, ow, od, z, gm.reshape(1, -1), w_out, n1.reshape(1, -1))


def _xattn_kernel(x_ref, n2_ref, wq_ref, kv_ref, wo_ref, n3_ref, y_ref):
    x = x_ref[0]
    q = _dot(_rms(x, n2_ref[...]).astype(BF16), wq_ref[...])
    kv = kv_ref[0].astype(BF16)
    hw = N_HEADS * HEAD_DIM
    o = jnp.zeros(x.shape, F32)
    for h in range(N_HEADS):
        cols = slice(h * HEAD_DIM, (h + 1) * HEAD_DIM)
        s = _dot_nt(q[:, cols].astype(BF16), kv[:, cols]) * SCALE
        e = jnp.exp(s - jnp.max(s, axis=-1, keepdims=True))
        p = e / jnp.sum(e, axis=-1, keepdims=True)
        oh = _dot(p.astype(BF16), kv[:, hw + h * HEAD_DIM:hw + (h + 1) * HEAD_DIM])
        o = o + _dot(oh.astype(BF16), wo_ref[cols, :])
    y_ref[0] = x + _rms(o, n3_ref[...])


def xattn(x, n2, wq, kv, wo, n3, tm):
    g, m, _ = x.shape
    gk, mem, kvw = kv.shape
    kmap = (lambda a, i: (a, 0, 0)) if gk > 1 else (lambda a, i: (0, 0, 0))
    xs = pl.BlockSpec((1, tm, D_MODEL), lambda a, i: (a, i, 0))
    return pl.pallas_call(
        _xattn_kernel,
        grid=(g, m // tm),
        in_specs=[xs, _const_spec((1, D_MODEL)), _const_spec((D_MODEL, GROUP_W)), pl.BlockSpec((1, mem, kvw), kmap),
                  _const_spec((GROUP_W, D_MODEL)), _const_spec((1, D_MODEL))],
        out_specs=xs,
        out_shape=jax.ShapeDtypeStruct(x.shape, F32),
        compiler_params=_params("parallel", "parallel"),
        name="cross_attention",
    )(x, n2.reshape(1, -1), wq, kv, wo, n3.reshape(1, -1))


FF_CHUNK = 1024


def _ffn_kernel(x_ref, n4_ref, w1_ref, w2_ref, n5_ref, y_ref):
    x = x_ref[...]
    h = _rms(x, n4_ref[...]).astype(BF16)
    f = jnp.zeros(x.shape, F32)
    for c in range(D_FF // FF_CHUNK):
        cols = slice(c * FF_CHUNK, (c + 1) * FF_CHUNK)
        a = jnp.maximum(_dot(h, w1_ref[:, cols]), 0.0)
        f = f + _dot((a * a).astype(BF16), w2_ref[cols, :])
    y_ref[...] = x + _rms(f, n5_ref[...])


def ffn(x, n4, w1, w2, n5, tm):
    m = x.shape[0]
    row = pl.BlockSpec((tm, D_MODEL), lambda i: (i, 0))
    once = pl.Buffered(1)
    return pl.pallas_call(
        _ffn_kernel,
        grid=(m // tm,),
        in_specs=[row, _const_spec((1, D_MODEL)),
                  pl.BlockSpec((D_MODEL, D_FF), lambda i: (0, 0), pipeline_mode=once),
                  pl.BlockSpec((D_FF, D_MODEL), lambda i: (0, 0), pipeline_mode=once),
                  _const_spec((1, D_MODEL))],
        out_specs=row,
        out_shape=jax.ShapeDtypeStruct((m, D_MODEL), F32),
        compiler_params=_params("parallel"),
        name="ffn",
    )(x, n4.reshape(1, -1), w1, w2, n5.reshape(1, -1))


FOX_PG = 16


def _fox_sample_kernel(pt_ref, q_ref, cnb_ref, cnk_ref, kn_ref, vn_ref, u_ref, *refs, n_tok):
    kv_refs = refs[:FOX_PG]
    lf_refs = refs[FOX_PG:2 * FOX_PG]
    o_ref = refs[2 * FOX_PG]
    m_sc, acc_sc, carry_sc = refs[2 * FOX_PG + 1:]
    g = pl.program_id(1)
    rows = N_HEADS * n_tok
    q = q_ref[0]
    cnb = cnb_ref[0]

    def update(s, valid, pv):
        m_old = m_sc[...]
        m_new = jnp.maximum(m_old, jnp.max(s, axis=-1, keepdims=True))
        p = jnp.exp(s - m_new)
        if valid is not None:
            p = jnp.where(valid, p, 0.0)
        alpha = jnp.exp(m_old - m_new)
        l_add = jnp.sum(p, axis=-1, keepdims=True)
        acc = acc_sc[...]
        acc_sc[:, 0:GROUP_W] = alpha * acc[:, 0:GROUP_W] + pv(p.astype(BF16))
        acc_sc[:, GROUP_W:] = alpha * acc[:, GROUP_W:] + l_add
        m_sc[...] = m_new

    @pl.when(g == 0)
    def _():
        m_sc[...] = jnp.full((rows, 1), NEG_BIG, F32)
        acc_sc[...] = jnp.zeros(acc_sc.shape, F32)
        carry_sc[...] = jnp.zeros(carry_sc.shape, F32)
        nk = kn_ref.shape[1]
        t_idx = lax.broadcasted_iota(I32, (rows, nk), 0) % n_tok
        j_idx = lax.broadcasted_iota(I32, (rows, nk), 1)
        valid = jnp.logical_and(j_idx <= t_idx, j_idx < n_tok)
        s = _dot_nt(q, kn_ref[0]) + cnb[:, 0:nk] - cnk_ref[0][:, 0:nk]
        vn = vn_ref[0]
        update(jnp.where(valid, s, NEG_BIG), valid, lambda p: _dot(p, vn))

    lf = jnp.concatenate([r[0, 0] for r in lf_refs], axis=0)
    suffix = _dot_exact_lhs(lf, u_ref[...])
    total = jnp.broadcast_to(jnp.sum(lf, axis=-1, keepdims=True), lf.shape)
    carry = carry_sc[...]
    scores, vts = [], []
    for i in range(FOX_PG):
        rows_i = slice(i * N_HEADS, (i + 1) * N_HEADS)
        kt = kv_refs[i][0, 0, 0].reshape(GROUP_W, PAGE).astype(BF16)
        vts.append(kv_refs[i][0, 0, 1].reshape(GROUP_W, PAGE).astype(BF16))
        b4 = suffix[rows_i] + carry
        bias = jnp.concatenate([jnp.broadcast_to(b4[h:h + 1], (n_tok, PAGE)) for h in range(N_HEADS)], axis=0)
        scores.append(_dot(q, kt) + bias + cnb)
        carry = carry + total[rows_i]
    carry_sc[...] = carry

    def pv_all(p):
        out = _dot_nt(p[:, 0:PAGE], vts[0])
        for i in range(1, FOX_PG):
            out = out + _dot_nt(p[:, i * PAGE:(i + 1) * PAGE], vts[i])
        return out

    update(jnp.concatenate(scores, axis=1), None, pv_all)

    @pl.when(g == pl.num_programs(1) - 1)
    def _():
        acc = acc_sc[...]
        o = acc[:, 0:GROUP_W] / acc[:, GROUP_W:GROUP_W + 1]
        r_h = lax.broadcasted_iota(I32, (rows, GROUP_W), 0) // n_tok
        c_h = lax.broadcasted_iota(I32, (rows, GROUP_W), 1) // HEAD_DIM
        o = jnp.where(r_h == c_h, o, 0.0)
        out = o[0:n_tok]
        for h in range(1, N_HEADS):
            out = out + o[h * n_tok:(h + 1) * n_tok]
        o_ref[0] = out


def fox_sample(page_table, qbd, cnb, cnk, knew, vnew, cache_kvt, cache_logft, *, layer, n_tok):
    bsz, n_pages = page_table.shape
    rows = N_HEADS * n_tok
    ngrp = n_pages // FOX_PG
    u = jnp.tril(jnp.ones((PAGE, PAGE), F32), -1).astype(BF16)

    def page_map(i, ndim):
        return lambda b, g, pt: (layer, pt[b, n_pages - 1 - (g * FOX_PG + i)]) + (0,) * (ndim - 2)

    per_b = lambda shape: pl.BlockSpec((1,) + shape, lambda b, g, pt: (b, 0, 0))
    grid_spec = pltpu.PrefetchScalarGridSpec(
        num_scalar_prefetch=1,
        grid=(bsz, ngrp),
        in_specs=[per_b((rows, GROUP_W)), per_b((rows, LANES)), per_b((rows, LANES)), per_b((8, GROUP_W)),
                  per_b((8, GROUP_W)), pl.BlockSpec((PAGE, PAGE), lambda b, g, pt: (0, 0))]
                 + [pl.BlockSpec((1, 1, 2, N_HEADS, HEAD_DIM, PAGE), page_map(i, 6)) for i in range(FOX_PG)]
                 + [pl.BlockSpec((1, 1, N_HEADS, PAGE), page_map(i, 4)) for i in range(FOX_PG)],
        out_specs=pl.BlockSpec((1, n_tok, GROUP_W), lambda b, g, pt: (b, 0, 0)),
        scratch_shapes=[pltpu.VMEM((rows, 1), F32), pltpu.VMEM((rows, GROUP_W + LANES), F32),
                        pltpu.VMEM((N_HEADS, PAGE), F32)],
    )
    return pl.pallas_call(
        functools.partial(_fox_sample_kernel, n_tok=n_tok),
        grid_spec=grid_spec,
        out_shape=jax.ShapeDtypeStruct((bsz, n_tok, GROUP_W), F32),
        compiler_params=_params("parallel", "arbitrary"),
        name="fox_sample",
    )(page_table, qbd, cnb, cnk, knew, vnew, u, *([cache_kvt] * FOX_PG), *([cache_logft] * FOX_PG))


CMP_PG = 16
CMP_PER_PAGE = PAGE // NSA_CMP_BLOCK


def _cmp_pages_kernel(pt_ref, w_ref, g_ref, *refs):
    o_ref = refs[CMP_PG]
    w = w_ref[...]
    x = jnp.concatenate([refs[i][0, 0].reshape(2 * HEAD_DIM, PAGE) * w for i in range(CMP_PG)], axis=1)
    o_ref[0, 0] = _dot_exact_lhs(x, g_ref[...])


def nsa_compress_pages(page_table, wt, cache_nsat, *, layer):
    bsz, n_pages = page_table.shape
    per_step = CMP_PG * CMP_PER_PAGE
    lane = np.arange(CMP_PG * PAGE)
    seg = jnp.asarray((lane // NSA_CMP_BLOCK)[:, None] == np.arange(per_step)[None, :], BF16)

    def page_map(i):
        return lambda b, g, pt: (layer, pt[b, g * CMP_PG + i], 0, 0, 0)

    grid_spec = pltpu.PrefetchScalarGridSpec(
        num_scalar_prefetch=1,
        grid=(bsz, n_pages // CMP_PG),
        in_specs=[pl.BlockSpec((2 * HEAD_DIM, PAGE), lambda b, g, pt: (0, 0)),
                  pl.BlockSpec(seg.shape, lambda b, g, pt: (0, 0))]
                 + [pl.BlockSpec((1, 1, 2, HEAD_DIM, PAGE), page_map(i)) for i in range(CMP_PG)],
        out_specs=pl.BlockSpec((1, 1, 2 * HEAD_DIM, per_step), lambda b, g, pt: (b, g, 0, 0)),
    )
    return pl.pallas_call(
        _cmp_pages_kernel,
        grid_spec=grid_spec,
        out_shape=jax.ShapeDtypeStruct((bsz, n_pages // CMP_PG, 2 * HEAD_DIM, per_step), F32),
        compiler_params=_params("parallel", "parallel"),
        name="nsa_compress_pages",
    )(page_table, wt, seg, *([cache_nsat] * CMP_PG))


def _slc_sample_kernel(idx_ref, pt_ref, q_ref, kn_ref, vn_ref, *refs, n_sel, n_tok):
    blk_refs = refs[:n_sel]
    o_ref = refs[n_sel]
    i = pl.program_id(0)
    t = i % n_tok
    q = q_ref[0]
    nq = q.shape[0]
    lane_half = lax.broadcasted_iota(I32, (nq, PAGE), 1) // NSA_SEL_BLOCK
    scores, masks = [], []
    for j in range(n_sel):
        half = idx_ref[i, j] % (PAGE // NSA_SEL_BLOCK)
        ok = lane_half == half
        s = _dot(q, blk_refs[j][0, 0, 0].astype(BF16)) * SCALE
        scores.append(jnp.where(ok, s, NEG_BIG))
        masks.append(ok)
    nk = kn_ref.shape[1]
    ok_new = lax.broadcasted_iota(I32, (nq, nk), 1) <= t
    s_new = jnp.where(ok_new, _dot_nt(q, kn_ref[0]) * SCALE, NEG_BIG)
    m = jnp.max(s_new, axis=-1, keepdims=True)
    for s in scores:
        m = jnp.maximum(m, jnp.max(s, axis=-1, keepdims=True))
    e_new = jnp.where(ok_new, jnp.exp(s_new - m), 0.0)
    es = [jnp.where(ok, jnp.exp(s - m), 0.0) for s, ok in zip(scores, masks)]
    d = jnp.sum(e_new, axis=-1, keepdims=True)
    for e in es:
        d = d + jnp.sum(e, axis=-1, keepdims=True)
    o = _dot((e_new / d).astype(BF16), vn_ref[0])
    for j, e in enumerate(es):
        o = o + _dot_nt((e / d).astype(BF16), blk_refs[j][0, 0, 1].astype(BF16))
    o_ref[0] = o


def slc_sample(idx, page_table, q, knew, vnew, cache_nsat, *, layer, n_sel, n_tok):
    n_q = q.shape[0]
    blocks_per_page = PAGE // NSA_SEL_BLOCK

    def blk_map(j):
        def f(i, idx_r, pt):
            return (layer, pt[i // n_tok, idx_r[i, j] // blocks_per_page], 1, 0, 0)
        return f

    per_b = lambda shape: pl.BlockSpec((1,) + shape, lambda i, idx_r, pt: (i // n_tok, 0, 0))
    grid_spec = pltpu.PrefetchScalarGridSpec(
        num_scalar_prefetch=2,
        grid=(n_q,),
        in_specs=[pl.BlockSpec((1, 8, HEAD_DIM), lambda i, idx_r, pt: (i, 0, 0)), per_b((8, HEAD_DIM)),
                  per_b((8, HEAD_DIM))]
                 + [pl.BlockSpec((1, 1, 2, HEAD_DIM, PAGE), blk_map(j)) for j in range(n_sel)],
        out_specs=pl.BlockSpec((1, 8, HEAD_DIM), lambda i, idx_r, pt: (i, 0, 0)),
    )
    return pl.pallas_call(
        functools.partial(_slc_sample_kernel, n_sel=n_sel, n_tok=n_tok),
        grid_spec=grid_spec,
        out_shape=jax.ShapeDtypeStruct((n_q, 8, HEAD_DIM), F32),
        compiler_params=_params("parallel"),
        name="nsa_slc_sample",
    )(idx, page_table, q, knew, vnew, *([cache_nsat] * n_sel))


def _win_sample_kernel(q_ref, k_ref, v_ref, o_ref, *, n_tok, n_win, n_keys):
    q = q_ref[0]
    rows, width = q.shape[0], k_ref.shape[1]
    t = lax.broadcasted_iota(I32, (rows, width), 0) % n_tok
    j = lax.broadcasted_iota(I32, (rows, width), 1)
    dist = n_win + t - j
    valid = jnp.logical_and(jnp.logical_and(dist >= 0, dist < NSA_WINDOW), j < n_keys)
    s = jnp.where(valid, _dot_nt(q, k_ref[0]) * SCALE, NEG_BIG)
    m = jnp.max(s, axis=-1, keepdims=True)
    e = jnp.where(valid, jnp.exp(s - m), 0.0)
    d = jnp.sum(e, axis=-1, keepdims=True)
    o_ref[0] = _dot((e / jnp.where(d > 0.0, d, 1.0)).astype(BF16), v_ref[0])


def win_sample(q, kw, vw, *, n_tok, n_win, n_keys):
    bsz, rows, _ = q.shape
    width = kw.shape[1]
    spec = lambda r: pl.BlockSpec((1, r, HEAD_DIM), lambda b: (b, 0, 0))
    return pl.pallas_call(
        functools.partial(_win_sample_kernel, n_tok=n_tok, n_win=n_win, n_keys=n_keys),
        grid=(bsz,),
        in_specs=[spec(rows), spec(width), spec(width)],
        out_specs=spec(rows),
        out_shape=jax.ShapeDtypeStruct((bsz, rows, HEAD_DIM), F32),
        compiler_params=_params("parallel"),
        name="nsa_win_sample",
    )(q, kw, vw)


def _split3_host(x):
    bits = lambda a: lax.bitcast_convert_type(
        lax.bitcast_convert_type(a, jnp.uint32) & jnp.uint32(0xFFFF0000), F32)
    hi = bits(x)
    r = x - hi
    mid = bits(r)
    lo = bits(r - mid)
    return hi.astype(BF16), mid.astype(BF16), lo.astype(BF16)


def _heads_t(a):
    return a.reshape(a.shape[0], N_HEADS, HEAD_DIM).transpose(1, 2, 0)


def _pad_rows(parts, height, dtype):
    used = sum(p.shape[-2] for p in parts)
    pad = jnp.zeros(parts[0].shape[:-2] + (height - used, parts[0].shape[-1]), dtype)
    return jnp.concatenate([p.astype(dtype) for p in parts] + [pad], axis=-2)


def _kv_tiles(k_rows, vt, tk):
    g, length, kd = k_rows.shape
    n = length // tk
    return (k_rows.reshape(g, n, tk, kd),
            vt.reshape(g, LANES, n, tk).transpose(0, 2, 1, 3))


def _q_cols_tiles(a, tq):
    length, _, w = a.shape
    return a.reshape(length // tq, tq, N_HEADS, w).transpose(3, 0, 2, 1).reshape(w, -1)


def _from_cols_tiles(o, tq):
    return o.reshape(HEAD_DIM, -1, N_HEADS, tq).transpose(1, 3, 2, 0).reshape(-1, GROUP_W)


def _evens_odds_cols(a):
    return jnp.concatenate([a[..., 0::2], a[..., 1::2]], axis=-1)


def _pick(n, candidates):
    for c in candidates:
        if n % c == 0:
            return c
    raise ValueError(f"no tile size for {n}")


def _layer_weights(w, l):
    w_in = w["w_in"][l]
    w_in_p = jnp.concatenate(
        [w_in[:, 0:768], w_in[:, 772:2052], w_in[:, 2448:2960], w_in[:, 2052:2436], w_in[:, 768:772],
         w_in[:, 2436:2448], jnp.zeros((D_MODEL, D_PROJ_PAD - 2960), F32)], axis=1).astype(BF16)
    bd = lambda m: jax.scipy.linalg.block_diag(*[m[i] for i in range(m.shape[0])]).astype(BF16)
    cmp_w = w["nsa_cmp_w"][l]
    return dict(
        w_in=w_in_p, n=w["norms"][l], b_fox=w["b_fox_f"][l],
        w2cmp=jnp.concatenate([cmp_w[0], cmp_w[1]], axis=-1),
        wtcmp=jnp.tile(cmp_w.transpose(0, 2, 1), (1, 1, CMP_PER_PAGE)).reshape(2 * HEAD_DIM, PAGE),
        cw=w["lru_conv_w"][l], cb=w["lru_conv_b"][l], wa=bd(w["lru_wa"][l]), ba=w["lru_ba"][l],
        wx=bd(w["lru_wx"][l]), bx=w["lru_bx"][l], lam=w["lru_lambda"][l], gm=w["g_mix"][l],
        w_out=w["w_out"][l].astype(BF16), g_mem=w["g_mem"][l], wq=w["xa_wq"][l].astype(BF16),
        wkv=w["xa_wkv"][l].astype(BF16), wo=w["xa_wo"][l].astype(BF16), w1=w["w_ff1"][l].astype(BF16),
        w2=w["w_ff2"][l].astype(BF16))


def _prompt_layer(x, mem, wl, l, hgrn_lower):
    length = x.shape[0]
    tm = _pick(length, (512, 256, 128))
    z = rms_matmul(x, wl["n"][0], wl["w_in"], tm)

    ff = z[:, C_FF:C_FF + N_HEADS].T.reshape(N_HEADS, length // LANES, LANES)
    logf, c = logf_cumsum(ff, wl["b_fox"], valid=LANES, chain=True)
    logf = logf.reshape(N_HEADS, length)
    c3 = _split3_host(c.reshape(N_HEADS, length) * LOG2E)
    c3_rows = [p[:, None, :] for p in c3]
    c3_cols = [-p[:, :, None] for p in c3]
    ones_rows = jnp.ones((N_HEADS, 3, length), BF16)
    ones_cols = jnp.ones((N_HEADS, length, 3), BF16)
    t_a = _pick(length, (512, 256, 128))
    q_a = _pad_rows([_heads_t(z[:, C_FQ:C_FQ + GROUP_W] * (SCALE * LOG2E))] + c3_rows + [ones_rows], LANES, BF16)
    k_rows = jnp.concatenate(
        [_heads_t(z[:, C_FK:C_FK + GROUP_W]).transpose(0, 2, 1).astype(BF16), ones_cols] + c3_cols
        + [jnp.zeros((N_HEADS, length, LANES - HEAD_DIM - 6), BF16)], axis=-1)
    v_t = _pad_rows([_heads_t(z[:, C_FV:C_FV + GROUP_W]), jnp.ones((N_HEADS, 1, length), BF16)], LANES, BF16)
    o_a = flash_attention(q_a, *_kv_tiles(k_rows, v_t, t_a), tq=t_a, hstack=1)
    o_a = o_a.transpose(2, 0, 1).reshape(length, GROUP_W)

    chunk = 64
    n_chunks = _pick(length // chunk, (8, 4, 2, 1))
    s0t = jnp.zeros((1, N_HEADS, HEAD_DIM, HEAD_DIM), F32)
    o_b, st = hgrn(z[None], hgrn_lower, wl["gm"][GROUP_W:2 * GROUP_W], s0t, layer=l, chunk=chunk,
                   n_chunks=n_chunks, valid_len=length)
    o_b = o_b[0]
    s_fin = st.transpose(0, 1, 3, 2)

    tq = 128
    kvcmp = nsa_compress(z, wl["w2cmp"], _pick(length, (1024, 512, 256, 128)))
    ns = length // NSA_SEL_BLOCK
    nq = z[:, C_NQ:C_NQ + GROUP_W]
    o_c, sel_t, _ = cmp_topk(nq.T[None], _evens_odds_cols(kvcmp.T)[None], tq=tq, pos0=0,
                             n_rounds=min(NSA_TOPK, ns))
    o_c = o_c[0].T
    n_win = -(-ns // SEL_WINDOW)
    sel_t = jnp.pad(sel_t, ((0, 0), (0, n_win * SEL_WINDOW - ns), (0, 0)))
    q_n = _pad_rows([_q_cols_tiles(nq.reshape(length, N_HEADS, HEAD_DIM) * (SCALE * LOG2E), tq)], LANES, BF16)
    nkv = z[:, C_NKV:C_NKV + 6 * HEAD_DIM]
    onehot = ((jnp.arange(length)[:, None] // NSA_SEL_BLOCK) % SEL_WINDOW == jnp.arange(SEL_WINDOW)[None, :])
    z64 = jnp.zeros((length, HEAD_DIM), BF16)
    one_row = jnp.ones((1, length), BF16)
    k_s = jnp.concatenate([nkv[:, 128:192].astype(BF16), z64, onehot.astype(BF16)], axis=-1)
    v_s = _pad_rows([nkv[:, 192:256].T, one_row], LANES, BF16)
    tk_s = _pick(length, (512, 256, 128))
    o_s = flash_attention(q_n[None], *_kv_tiles(k_s[None], v_s[None], tk_s), sel_t, tq=tq, hstack=N_HEADS)
    o_s = _from_cols_tiles(o_s[0], tq)
    k_w = jnp.concatenate([nkv[:, 256:320].astype(BF16), z64], axis=-1)
    v_w = _pad_rows([nkv[:, 320:384].T, one_row], LANES, BF16)
    o_w = flash_attention(q_n[None], *_kv_tiles(k_w[None], v_w[None], tq), tq=tq, hstack=N_HEADS,
                          window=NSA_WINDOW)
    o_w = _from_cols_tiles(o_w[0], tq)

    conv0 = jnp.zeros((1, LRU_PAD, GROUP_W), F32)
    h0 = jnp.zeros((1, 1, GROUP_W), F32)
    o_d, h_last = rglru(z[None], conv0, h0, wl["cw"], wl["cb"], wl["wa"], wl["ba"], wl["wx"], wl["bx"],
                        wl["lam"], tb=_pick(length, (512, 256, 128)))

    y = mix(x, o_a, o_b, o_c, o_s, o_w, o_d[0], z, wl["gm"], wl["w_out"], wl["n"][1], tm)
    xa_kv = rms_matmul(mem, wl["g_mem"], wl["wkv"], mem.shape[0])
    y = xattn(y[None], wl["n"][2], wl["wq"], xa_kv[None], wl["wo"], wl["n"][3], tm)[0]
    y = ffn(y, wl["n"][4], wl["w1"], wl["w2"], wl["n"][5], _pick(length, (256, 128)))

    wb = min(NSA_WINDOW, length)
    state = (
        z[:, C_FK:C_FK + 2 * GROUP_W].reshape(1, length, 2, N_HEADS, HEAD_DIM),
        logf.T[None],
        nkv[:, :4 * HEAD_DIM].reshape(1, length, 4, HEAD_DIM),
        nkv[length - wb:, 4 * HEAD_DIM:].reshape(1, wb, 2, HEAD_DIM),
        s_fin,
        h_last[:, 0],
        z[length - (LRU_CONV - 1):, C_UX:C_UX + GROUP_W][None],
        xa_kv.reshape(1, mem.shape[0], 2, N_HEADS, HEAD_DIM),
    )
    return y, state


def _sample_layer(x, wl, l, hgrn_lower, page_table, cache_fox_kvt, cache_fox_logft, cache_nsat, win_buf,
                  s0, h0, conv_buf, xa_kv):
    bsz, n_tok, _ = x.shape
    rows = bsz * n_tok
    n_pages = page_table.shape[1]
    past = n_pages * PAGE
    z = rms_matmul(x.reshape(rows, D_MODEL), wl["n"][0], wl["w_in"], rows)
    zb = z.reshape(bsz, n_tok, D_PROJ_PAD)

    ff = jnp.pad(zb[:, :, C_FF:C_FF + N_HEADS].transpose(2, 0, 1), ((0, 0), (0, 0), (0, LANES - n_tok)))
    logf, cn = logf_cumsum(ff, wl["b_fox"], valid=n_tok, chain=False)
    logf_new = logf[:, :, :n_tok].transpose(1, 2, 0)
    cn_t = cn[:, :, :n_tok].transpose(1, 0, 2)
    cnb = jnp.broadcast_to(cn_t.reshape(bsz, N_HEADS * n_tok, 1), (bsz, N_HEADS * n_tok, LANES))
    cnk = jnp.pad(jnp.repeat(cn_t, n_tok, axis=1), ((0, 0), (0, 0), (0, LANES - n_tok)))
    fq = zb[:, :, C_FQ:C_FQ + GROUP_W] * SCALE
    head_mask = (jnp.arange(N_HEADS)[:, None] == jnp.arange(GROUP_W)[None, :] // HEAD_DIM)
    qbd = jnp.where(head_mask[None, :, None, :], fq[:, None], 0.0).reshape(bsz, N_HEADS * n_tok, GROUP_W)
    pad_tok = ((0, 0), (0, 8 - n_tok), (0, 0))
    knew = jnp.pad(zb[:, :, C_FK:C_FK + GROUP_W], pad_tok).astype(BF16)
    vnew = jnp.pad(zb[:, :, C_FV:C_FV + GROUP_W], pad_tok).astype(BF16)
    o_a = fox_sample(page_table, qbd.astype(BF16), cnb, cnk, knew, vnew, cache_fox_kvt, cache_fox_logft,
                     layer=l, n_tok=n_tok)

    chunk = 8
    zp = jnp.pad(zb, ((0, 0), (0, chunk - n_tok), (0, 0)))
    o_b, st = hgrn(zp, hgrn_lower, wl["gm"][GROUP_W:2 * GROUP_W], s0.transpose(0, 1, 3, 2), layer=l,
                   chunk=chunk, n_chunks=1, valid_len=n_tok)
    o_b = o_b[:, :n_tok]
    s_fin = st.transpose(0, 1, 3, 2)

    nkv = zb[:, :, C_NKV:C_NKV + 6 * HEAD_DIM]
    nq = zb[:, :, C_NQ:C_NQ + GROUP_W]
    kvcmp = nsa_compress_pages(page_table, wl["wtcmp"], cache_nsat, layer=l)
    kvcmp = kvcmp.transpose(0, 2, 1, 3).reshape(bsz, 2 * HEAD_DIM, past // NSA_CMP_BLOCK)
    n_in = min(NSA_TOPK, past // NSA_SEL_BLOCK + 1) - 1
    o_c, _, idx = cmp_topk(nq.transpose(0, 2, 1), _evens_odds_cols(kvcmp), tq=n_tok, pos0=past, n_rounds=n_in)
    o_c = o_c.transpose(0, 2, 1)
    idx = idx.transpose(0, 2, 1).reshape(rows, NSA_TOPK)
    q_rows = jnp.pad(nq.reshape(rows, N_HEADS, HEAD_DIM), ((0, 0), (0, 8 - N_HEADS), (0, 0))).astype(BF16)
    ks_new = jnp.pad(nkv[:, :, 128:192], pad_tok).astype(BF16)
    vs_new = jnp.pad(nkv[:, :, 192:256], pad_tok).astype(BF16)
    o_s = slc_sample(idx, page_table, q_rows, ks_new, vs_new, cache_nsat, layer=l, n_sel=n_in, n_tok=n_tok)
    o_s = o_s[:, :N_HEADS].reshape(bsz, n_tok, GROUP_W)
    n_win = win_buf.shape[1]
    win = jnp.concatenate([win_buf, nkv[:, :, 256:384].reshape(bsz, n_tok, 2, HEAD_DIM)], axis=1)
    n_keys = n_win + n_tok
    kpad = -(-n_keys // 8) * 8
    win_p = jnp.pad(win, ((0, 0), (0, kpad - n_keys), (0, 0), (0, 0))).astype(BF16)
    q_ht = nq.reshape(bsz, n_tok, N_HEADS, HEAD_DIM).transpose(0, 2, 1, 3).reshape(bsz, N_HEADS * n_tok, HEAD_DIM)
    o_w = win_sample(q_ht.astype(BF16), win_p[:, :, 0], win_p[:, :, 1], n_tok=n_tok, n_win=n_win, n_keys=n_keys)
    o_w = o_w.reshape(bsz, N_HEADS, n_tok, HEAD_DIM).transpose(0, 2, 1, 3).reshape(bsz, n_tok, GROUP_W)

    conv0 = jnp.pad(conv_buf, ((0, 0), (LRU_PAD - (LRU_CONV - 1), 0), (0, 0)))
    o_d, h_last = rglru(zb, conv0, h0[:, None], wl["cw"], wl["cb"], wl["wa"], wl["ba"], wl["wx"], wl["bx"],
                        wl["lam"], tb=n_tok)

    flat = lambda a: a.reshape(rows, GROUP_W)
    y = mix(x.reshape(rows, D_MODEL), flat(o_a), flat(o_b), flat(o_c), flat(o_s), flat(o_w), flat(o_d), z,
            wl["gm"], wl["w_out"], wl["n"][1], rows)
    mem_len = xa_kv.shape[1]
    y = xattn(y.reshape(bsz, n_tok, D_MODEL), wl["n"][2], wl["wq"], xa_kv.reshape(bsz, mem_len, 2 * GROUP_W),
              wl["wo"], wl["n"][3], n_tok)
    y = ffn(y.reshape(rows, D_MODEL), wl["n"][4], wl["w1"], wl["w2"], wl["n"][5], rows)

    state = (
        zb[:, :, C_FK:C_FK + 2 * GROUP_W].reshape(bsz, n_tok, 2, N_HEADS, HEAD_DIM),
        logf_new,
        nkv[:, :, :4 * HEAD_DIM].reshape(bsz, n_tok, 4, HEAD_DIM),
        win[:, n_tok:],
        s_fin,
        h_last[:, 0],
        jnp.concatenate([conv_buf, zb[:, :, C_UX:C_UX + GROUP_W]], axis=1)[:, n_tok:],
    )
    return y.reshape(bsz, n_tok, D_MODEL), state


def kernel(x_prompt, mem_prompt, x_sample, cache_fox_kv, cache_fox_logf, cache_nsa_kv, state_nsa_win, state_hgrn,
           state_lru_h, state_lru_conv, cache_xa_kv, page_table, w_in, b_fox_f, hgrn_lower, nsa_cmp_w, lru_conv_w,
           lru_conv_b, lru_wa, lru_ba, lru_wx, lru_bx, lru_lambda, g_mix, w_out, g_mem, xa_wq, xa_wkv, xa_wo,
           w_ff1, w_ff2, norms):
    assert x_prompt.shape[0] == 1 and mem_prompt.shape[0] == 1
    w = dict(w_in=w_in, b_fox_f=b_fox_f, nsa_cmp_w=nsa_cmp_w, lru_conv_w=lru_conv_w, lru_conv_b=lru_conv_b,
             lru_wa=lru_wa, lru_ba=lru_ba, lru_wx=lru_wx, lru_bx=lru_bx, lru_lambda=lru_lambda, g_mix=g_mix,
             w_out=w_out, g_mem=g_mem, xa_wq=xa_wq, xa_wkv=xa_wkv, xa_wo=xa_wo, w_ff1=w_ff1, w_ff2=w_ff2,
             norms=norms)
    depth = w_in.shape[0]
    fox_kvt = cache_fox_kv.transpose(0, 1, 3, 4, 5, 2)
    fox_logft = cache_fox_logf.transpose(0, 1, 3, 2)
    nsat = cache_nsa_kv.transpose(0, 1, 3, 4, 2)
    y_p, y_s = x_prompt[0], x_sample
    p_out, s_out = [], []
    for l in range(depth):
        wl = _layer_weights(w, l)
        y_p, po = _prompt_layer(y_p, mem_prompt[0], wl, l, hgrn_lower)
        y_s, so = _sample_layer(y_s, wl, l, hgrn_lower, page_table, fox_kvt, fox_logft, nsat,
                                state_nsa_win[l], state_hgrn[l], state_lru_h[l], state_lru_conv[l], cache_xa_kv[l])
        p_out.append(po)
        s_out.append(so)
    stack = lambda outs, i: jnp.stack([o[i] for o in outs], axis=0)
    return ((y_p[None], y_s) + tuple(stack(p_out, i) for i in range(8)) + tuple(stack(s_out, i) for i in range(7)))
```
